```python
import jax, jax.numpy as jnp
from jax import lax
import numpy as np

D_MODEL = 1024
BATCH = 2
SEQ = 16384
DEPTH = 4

HEAD_DIM = 64
N_ATTN_HEADS = 8
N_IDX_HEADS = 4
IDX_DIM = 64
TOPK_MAX = 256
POOL_WINDOWS = (2, 4, 8, 16)
N_POOL_GROUPS = 4
POOL_GROUP_DIM = 64
N_MEM_HEADS = 4
N_MEM = 256
D_FF = 2816
Q_BLOCK = 128
EPS = 1e-6

ATTN_W = N_ATTN_HEADS * HEAD_DIM
POOL_W = N_POOL_GROUPS * POOL_GROUP_DIM
MEM_W = N_MEM_HEADS * HEAD_DIM
D_MIX = ATTN_W + POOL_W + MEM_W
IN_SIZES = (ATTN_W, ATTN_W, ATTN_W, N_IDX_HEADS * IDX_DIM, IDX_DIM, N_IDX_HEADS, POOL_W, MEM_W)
IN_OFFSETS = tuple(sum(IN_SIZES[:i + 1]) for i in range(len(IN_SIZES) - 1))
D_IN = sum(IN_SIZES)

kernel_name = "hybrid_dsa_pool_memory_macaron"


def rms_norm(x, g):
    xf = x.astype(jnp.float32)
    y = xf * lax.rsqrt(jnp.mean(xf * xf, axis=-1, keepdims=True) + EPS)
    return (y * g.astype(jnp.float32)).astype(x.dtype)


def swiglu(h, w_gate, w_up, w_down):
    return (jax.nn.silu(h @ w_gate) * (h @ w_up)) @ w_down


def dsa_attention(q, k, v, iq, ik, iw):
    B, S = q.shape[0], q.shape[1]
    n_sel = min(TOPK_MAX, S // 4)
    nb = S // Q_BLOCK
    pos = jnp.arange(S, dtype=jnp.int32)

    def to_blocks(a):
        return jnp.moveaxis(a.reshape((B, nb, Q_BLOCK) + a.shape[2:]), 1, 0)

    def block(args):
        qb, iqb, iwb, tb = args
        dots = jnp.einsum('bqhd,bsd->bqhs', iqb, ik).astype(jnp.float32) * (IDX_DIM ** -0.5)
        score = jnp.einsum('bqhs,bqh->bqs', jax.nn.relu(dots),
                           iwb.astype(jnp.float32) * (N_IDX_HEADS ** -0.5))
        score = jnp.where(pos[None, None, :] <= tb[None, :, None], score, -jnp.inf)
        _, idx = lax.top_k(score, n_sel)
        valid = idx <= tb[None, :, None]
        kg = jax.vmap(lambda kb, ib: kb[ib])(k, idx)
        vg = jax.vmap(lambda vb, ib: vb[ib])(v, idx)
        logits = jnp.einsum('bqhd,bqkhd->bhqk', qb, kg).astype(jnp.float32) * (HEAD_DIM ** -0.5)
        logits = jnp.where(valid[:, None], logits, -jnp.inf)
        p = jax.nn.softmax(logits, axis=-1).astype(vg.dtype)
        return jnp.einsum('bhqk,bqkhd->bqhd', p, vg)

    out = lax.map(block, (to_blocks(q), to_blocks(iq), to_blocks(iw), pos.reshape(nb, Q_BLOCK)))
    return jnp.moveaxis(out, 0, 1).reshape(B, S, -1)


def multiscale_pool(u, w_pool, scale):
    B, S = u.shape[0], u.shape[1]
    uf = u.astype(jnp.float32)
    c = jnp.cumsum(uf, axis=1)
    t1 = jnp.arange(1, S + 1, dtype=jnp.float32)
    means = []
    for g, win in enumerate(POOL_WINDOWS):
        cg = c[:, :, g]
        lagged = jnp.pad(cg, ((0, 0), (win, 0), (0, 0)))[:, :S]
        means.append((cg - lagged) / jnp.minimum(t1, float(win))[None, :, None])
    pooled = (jnp.stack(means, axis=2) - uf).astype(u.dtype)
    y = jnp.einsum('bsgc,gcd->bsgd', pooled, w_pool)
    return y.reshape(B, S, POOL_W) * scale


def memory_attention(mq, mem_n, w_kv, mq_g, mk_g):
    B, M = mem_n.shape[0], mem_n.shape[1]
    mk, mv = jnp.split(mem_n @ w_kv, 2, axis=-1)
    mk = rms_norm(mk.reshape(B, M, N_MEM_HEADS, HEAD_DIM), mk_g)
    mv = mv.reshape(B, M, N_MEM_HEADS, HEAD_DIM)
    mq = rms_norm(mq, mq_g)
    logits = jnp.einsum('bshd,bmhd->bhsm', mq, mk).astype(jnp.float32) * (HEAD_DIM ** -0.5)
    p = jax.nn.softmax(logits, axis=-1).astype(mv.dtype)
    out = jnp.einsum('bhsm,bmhd->bshd', p, mv)
    return out.reshape(out.shape[0], out.shape[1], MEM_W)


def setup_inputs(seed: int = 0) -> dict:
    key = jax.random.key(seed)
    ks = jax.random.split(key, 24)
    f32 = jnp.float32

    def nrm(k, shape, fan_in):
        return jax.random.normal(k, shape, f32) * (fan_in ** -0.5)

    def gain(k, shape):
        return 1.0 + 0.02 * jax.random.normal(k, shape, f32)

    return {
        "x": jax.random.normal(ks[0], (BATCH, SEQ, D_MODEL), f32),
        "mem": jax.random.normal(ks[1], (BATCH, N_MEM, D_MODEL), f32),
        "ffn1_norm": gain(ks[2], (DEPTH, D_MODEL)),
        "ffn1_gate": nrm(ks[3], (DEPTH, D_MODEL, D_FF), D_MODEL),
        "ffn1_up": nrm(ks[4], (DEPTH, D_MODEL, D_FF), D_MODEL),
        "ffn1_down": nrm(ks[5], (DEPTH, D_FF, D_MODEL), D_FF),
        "mix_norm": gain(ks[6], (DEPTH, D_MODEL)),
        "mem_norm": gain(ks[7], (DEPTH, D_MODEL)),
        "w_in": nrm(ks[8], (DEPTH, D_MODEL, D_IN), D_MODEL),
        "q_norm": gain(ks[9], (DEPTH, HEAD_DIM)),
        "k_norm": gain(ks[10], (DEPTH, HEAD_DIM)),
        "w_pool": nrm(ks[11], (DEPTH, N_POOL_GROUPS, POOL_GROUP_DIM, POOL_GROUP_DIM), POOL_GROUP_DIM),
        "pool_scale": gain(ks[12], (DEPTH, POOL_W)),
        "w_mem_kv": nrm(ks[13], (DEPTH, D_MODEL, 2 * MEM_W), D_MODEL),
        "mq_norm": gain(ks[14], (DEPTH, HEAD_DIM)),
        "mk_norm": gain(ks[15], (DEPTH, HEAD_DIM)),
        "w_out": nrm(ks[16], (DEPTH, D_MIX, D_MODEL), D_MIX),
        "ffn2_norm": gain(ks[17], (DEPTH, D_MODEL)),
        "ffn2_gate": nrm(ks[18], (DEPTH, D_MODEL, D_FF), D_MODEL),
        "ffn2_up": nrm(ks[19], (DEPTH, D_MODEL, D_FF), D_MODEL),
        "ffn2_down": nrm(ks[20], (DEPTH, D_FF, D_MODEL), D_FF),
    }


def reference(x, mem, ffn1_norm, ffn1_gate, ffn1_up, ffn1_down, mix_norm, mem_norm, w_in,
              q_norm, k_norm, w_pool, pool_scale, w_mem_kv, mq_norm, mk_norm, w_out,
              ffn2_norm, ffn2_gate, ffn2_up, ffn2_down):
    B, S = x.shape[0], x.shape[1]
    for l in range(DEPTH):
        x = x + 0.5 * swiglu(rms_norm(x, ffn1_norm[l]), ffn1_gate[l], ffn1_up[l], ffn1_down[l])

        h = rms_norm(x, mix_norm[l])
        z = h @ w_in[l]
        aq, ak, av, iq, ik, iw, pu, mq = jnp.split(z, IN_OFFSETS, axis=-1)

        aq = rms_norm(aq.reshape(B, S, N_ATTN_HEADS, HEAD_DIM), q_norm[l])
        ak = rms_norm(ak.reshape(B, S, N_ATTN_HEADS, HEAD_DIM), k_norm[l])
        av = av.reshape(B, S, N_ATTN_HEADS, HEAD_DIM)
        iq = iq.reshape(B, S, N_IDX_HEADS, IDX_DIM)
        y_attn = dsa_attention(aq, ak, av, iq, ik, iw)

        y_pool = multiscale_pool(pu.reshape(B, S, N_POOL_GROUPS, POOL_GROUP_DIM),
                                 w_pool[l], pool_scale[l])

        y_mem = memory_attention(mq.reshape(B, S, N_MEM_HEADS, HEAD_DIM),
                                 rms_norm(mem, mem_norm[l]), w_mem_kv[l],
                                 mq_norm[l], mk_norm[l])

        x = x + jnp.concatenate([y_attn, y_pool, y_mem], axis=-1) @ w_out[l]

        x = x + 0.5 * swiglu(rms_norm(x, ffn2_norm[l]), ffn2_gate[l], ffn2_up[l], ffn2_down[l])
    return x
```

```python
import functools

import jax
import jax.numpy as jnp
from jax import lax
from jax.experimental import pallas as pl
from jax.experimental.pallas import tpu as pltpu

F32 = jnp.float32
BF16 = jnp.bfloat16
I32 = jnp.int32

EPS = 1e-6
HEAD_DIM = 64
N_ATTN_HEADS = 8
N_IDX_HEADS = 4
N_POOL_GROUPS = 4
N_MEM_HEADS = 4
POOL_WINDOWS = (2, 4, 8, 16)
TOPK_MAX = 256

ATTN_W = N_ATTN_HEADS * HEAD_DIM
IDX_W = N_IDX_HEADS * HEAD_DIM
POOL_W = N_POOL_GROUPS * HEAD_DIM
MEM_W = N_MEM_HEADS * HEAD_DIM
IN_SIZES = (ATTN_W, ATTN_W, ATTN_W, IDX_W, HEAD_DIM, N_IDX_HEADS, POOL_W, MEM_W)

LANES = 128
MXU_N = 256
VMEM_LIMIT_BYTES = 60 * 1024 * 1024
INT_MIN = -(2 ** 31)
NEG_BIG = -1e30

OFF_Q, OFF_K, OFF_V = 0, ATTN_W, 2 * ATTN_W
OFF_IQ = 3 * ATTN_W
OFF_IK = OFF_IQ + IDX_W
OFF_PU = OFF_IK + LANES
OFF_MQ = OFF_PU + POOL_W
OFF_IW = OFF_MQ + MEM_W
D_IN_PAD = OFF_IW + LANES

NT_DIMS = (((1,), (1,)), ((), ()))


def _rms(x, g):
    return x * lax.rsqrt(jnp.mean(x * x, axis=-1, keepdims=True) + EPS) * g


def _head_rms(v, gain, bd):
    sq = v * v
    hi = sq.astype(BF16)
    lo = (sq - hi.astype(F32)).astype(BF16)
    ms = (jnp.dot(hi, bd, preferred_element_type=F32)
          + jnp.dot(lo, bd, preferred_element_type=F32))
    return v * lax.rsqrt(ms + EPS) * gain


def _half_masks(rows):
    lane = lax.broadcasted_iota(I32, (rows, LANES), 1)
    lo = lane < HEAD_DIM
    lo_b = jnp.where(lo, 1.0, 0.0).astype(BF16)
    hi_b = jnp.where(lo, 0.0, 1.0).astype(BF16)
    return lo, (lo_b, hi_b)


def _ffn_kernel(*refs, mixed, n_chunks):
    if mixed:
        (x_ref, ya_ref, yb_ref, wo_ref, g_ref, wg_ref, wu_ref, wd_ref,
         o_ref, h_ref, acc_ref) = refs
        x = (x_ref[...]
             + jnp.dot(ya_ref[...], wo_ref[:ATTN_W, :], preferred_element_type=F32)
             + jnp.dot(yb_ref[...], wo_ref[ATTN_W:, :], preferred_element_type=F32))
    else:
        x_ref, g_ref, wg_ref, wu_ref, wd_ref, o_ref, h_ref, acc_ref = refs
        x = x_ref[...]
    o_ref[...] = x
    h_ref[...] = _rms(x, g_ref[...]).astype(BF16)
    acc_ref[...] = jnp.zeros_like(acc_ref)

    def chunk(c, carry):
        h = h_ref[...]
        g = jnp.dot(h, wg_ref[c], preferred_element_type=F32)
        u = jnp.dot(h, wu_ref[c], preferred_element_type=F32)
        a = (g * (1.0 / (1.0 + jnp.exp(-g))) * u).astype(BF16)
        acc_ref[...] += jnp.dot(a, wd_ref[c], preferred_element_type=F32)
        return carry

    lax.fori_loop(0, n_chunks, chunk, 0)
    o_ref[...] += 0.5 * acc_ref[...]


def _ffn_call(x2d, layer, gain, wg, wu, wd, mix=None, *, tm):
    n, d = x2d.shape
    n_chunks, _, fc = wg.shape[1:]
    const = lambda *shape: pl.BlockSpec((None,) + shape, lambda i: (layer,) + (0,) * len(shape),
                                        pipeline_mode=pl.Buffered(1))
    row = lambda w: pl.BlockSpec((tm, w), lambda i: (i, 0))
    in_specs, args = [row(d)], [x2d]
    if mix is not None:
        ya, yb, wo = mix
        in_specs += [row(ya.shape[1]), row(yb.shape[1]), const(*wo.shape[1:])]
        args += [ya, yb, wo]
    in_specs += [const(1, d), const(n_chunks, d, fc), const(n_chunks, d, fc), const(n_chunks, fc, d)]
    args += [gain, wg, wu, wd]
    return pl.pallas_call(
        functools.partial(_ffn_kernel, mixed=mix is not None, n_chunks=n_chunks),
        grid=(n // tm,),
        in_specs=in_specs,
        out_specs=row(d),
        out_shape=jax.ShapeDtypeStruct((n, d), F32),
        scratch_shapes=[pltpu.VMEM((tm, d), BF16), pltpu.VMEM((tm, d), F32)],
        compiler_params=pltpu.CompilerParams(
            dimension_semantics=("arbitrary",), vmem_limit_bytes=VMEM_LIMIT_BYTES),
        name="ffn_mixed" if mix is not None else "ffn",
    )(*args)


def _memkv_kernel(mem_ref, g_ref, w_ref, kg_ref, bd_ref, mk_ref, mv_ref):
    mem_n = _rms(mem_ref[...], g_ref[...]).astype(BF16)
    kv = jnp.dot(mem_n, w_ref[...], preferred_element_type=F32)
    mk_ref[...] = _head_rms(kv[:, :MEM_W], kg_ref[...], bd_ref[:MEM_W, :MEM_W]).astype(BF16)
    mv_ref[...] = kv[:, MEM_W:].astype(BF16)


def _memkv_call(mem, mem_norm, w_kv, mk_gain, bd):
    b, m, d = mem.shape
    depth = w_kv.shape[0]
    out = jax.ShapeDtypeStruct((depth, b, m, MEM_W), BF16)
    out_spec = pl.BlockSpec((None, None, m, MEM_W), lambda l, i: (l, i, 0, 0))
    return pl.pallas_call(
        _memkv_kernel,
        grid=(depth, b),
        in_specs=[
            pl.BlockSpec((None, m, d), lambda l, i: (i, 0, 0)),
            pl.BlockSpec((None, 1, d), lambda l, i: (l, 0, 0)),
            pl.BlockSpec((None, d, 2 * MEM_W), lambda l, i: (l, 0, 0)),
            pl.BlockSpec((None, 1, MEM_W), lambda l, i: (l, 0, 0)),
            pl.BlockSpec(bd.shape, lambda l, i: (0, 0)),
        ],
        out_specs=[out_spec, out_spec],
        out_shape=[out, out],
        compiler_params=pltpu.CompilerParams(dimension_semantics=("arbitrary", "arbitrary")),
        name="memkv",
    )(mem, mem_norm, w_kv, mk_gain, bd)


POOL_HALO = 32


def _proj_kernel(x_ref, g_ref, w_ref, qg_ref, kg_ref, mqg_ref, bd_ref, wp_ref, ps_ref,
                 mk_ref, mv_ref,
                 q_ref, k_ref, v_ref, iq_ref, ik_ref, iw_ref, y_ref,
                 p_ref, s2_ref, s4_ref, s8_ref, *, tm):
    j = pl.program_id(1)
    rows = tm + POOL_HALO
    h = _rms(x_ref[...], g_ref[...]).astype(BF16)
    z = jnp.dot(h, w_ref[...], preferred_element_type=F32)
    bd = bd_ref[...]

    q_ref[...] = (_head_rms(z[:, OFF_Q:OFF_Q + ATTN_W], qg_ref[...], bd)
                  * HEAD_DIM ** -0.5).astype(BF16)
    k_ref[...] = _head_rms(z[:, OFF_K:OFF_K + ATTN_W], kg_ref[...], bd).astype(BF16)
    v_ref[...] = z[:, OFF_V:OFF_V + ATTN_W].astype(BF16)
    iq_ref[...] = (z[:, OFF_IQ:OFF_IQ + IDX_W] * HEAD_DIM ** -0.5).astype(BF16)
    ik_ref[...] = z[:, OFF_IK:OFF_IK + LANES].astype(BF16)
    iw_ref[...] = z[:, OFF_IW:OFF_IW + LANES] * N_IDX_HEADS ** -0.5

    @pl.when(j == 0)
    def _():
        p_ref[0:POOL_HALO, :] = jnp.zeros((POOL_HALO, POOL_W), F32)

    pu = z[:, OFF_PU:OFF_PU + POOL_W]
    p_ref[POOL_HALO:rows, :] = pu
    s2_ref[8:rows, :] = p_ref[8:rows, :] + p_ref[7:rows - 1, :]
    s4_ref[16:rows, :] = s2_ref[16:rows, :] + s2_ref[14:rows - 2, :]
    s8_ref[24:rows, :] = s4_ref[24:rows, :] + s4_ref[20:rows - 4, :]
    s16 = s8_ref[POOL_HALO:rows, :] + s8_ref[24:rows - 8, :]
    lane = lax.broadcasted_iota(I32, (tm, POOL_W), 1)
    g1, g2, g3 = lane < HEAD_DIM, lane < 2 * HEAD_DIM, lane < 3 * HEAD_DIM
    wsum = jnp.where(g1, s2_ref[POOL_HALO:rows, :],
                     jnp.where(g2, s4_ref[POOL_HALO:rows, :],
                               jnp.where(g3, s8_ref[POOL_HALO:rows, :], s16)))
    win = jnp.where(g1, float(POOL_WINDOWS[0]),
                    jnp.where(g2, float(POOL_WINDOWS[1]),
                              jnp.where(g3, float(POOL_WINDOWS[2]), float(POOL_WINDOWS[3]))))
    t1 = (j * tm + 1 + lax.broadcasted_iota(I32, (tm, POOL_W), 0)).astype(F32)
    pooled = (wsum / jnp.minimum(t1, win) - pu).astype(BF16)
    y_pool = jnp.dot(pooled, wp_ref[...], preferred_element_type=F32) * ps_ref[...]
    y_ref[:, 0:POOL_W] = y_pool.astype(BF16)
    p_ref[16:POOL_HALO, :] = p_ref[tm + 16:rows, :]

    lo, half_b = _half_masks(tm)
    mqn = (_head_rms(z[:, OFF_MQ:OFF_MQ + MEM_W], mqg_ref[...], bd[:MEM_W, :MEM_W])
           * HEAD_DIM ** -0.5).astype(BF16)
    mk = mk_ref[...]
    mv = mv_ref[...]
    for pair in range(N_MEM_HEADS // 2):
        cols = slice(pair * LANES, (pair + 1) * LANES)
        outs = []
        for sub in range(2):
            logits = lax.dot_general(mqn[:, cols] * half_b[sub], mk[:, cols], NT_DIMS,
                                     preferred_element_type=F32)
            e = jnp.exp(logits - jnp.max(logits, axis=-1, keepdims=True))
            p = (e / jnp.sum(e, axis=-1, keepdims=True)).astype(BF16)
            outs.append(jnp.dot(p, mv[:, cols], preferred_element_type=F32))
        y_ref[:, POOL_W + pair * LANES:POOL_W + (pair + 1) * LANES] = (
            jnp.where(lo, outs[0], outs[1]).astype(BF16))


def _proj_call(x, layer, gain, w_in, qg, kg, mqg, bd, wp, ps, mk, mv, *, tm):
    b, s, d = x.shape
    m = mk.shape[2]
    lay = lambda *shape: pl.BlockSpec((None,) + shape, lambda i, j: (layer,) + (0,) * len(shape),
                                      pipeline_mode=pl.Buffered(1))
    tile = lambda w: pl.BlockSpec((None, tm, w), lambda i, j: (i, j, 0))
    mem = pl.BlockSpec((None, None, m, MEM_W), lambda i, j: (layer, i, 0, 0))
    widths = (ATTN_W, ATTN_W, ATTN_W, IDX_W, LANES, LANES, POOL_W + MEM_W)
    dtypes = (BF16, BF16, BF16, BF16, BF16, F32, BF16)
    rows = tm + POOL_HALO
    return pl.pallas_call(
        functools.partial(_proj_kernel, tm=tm),
        grid=(b, s // tm),
        in_specs=[tile(d), lay(1, d), lay(d, D_IN_PAD), lay(1, ATTN_W), lay(1, ATTN_W),
                  lay(1, MEM_W), pl.BlockSpec(bd.shape, lambda i, j: (0, 0)),
                  lay(POOL_W, POOL_W), lay(1, POOL_W), mem, mem],
        out_specs=[tile(w) for w in widths],
        out_shape=[jax.ShapeDtypeStruct((b, s, w), dt) for w, dt in zip(widths, dtypes)],
        scratch_shapes=[pltpu.VMEM((rows, POOL_W), F32) for _ in range(4)],
        compiler_params=pltpu.CompilerParams(
            dimension_semantics=("arbitrary", "arbitrary"), vmem_limit_bytes=VMEM_LIMIT_BYTES),
        name="proj",
    )(x, gain, w_in, qg, kg, mqg, bd, wp, ps, mk, mv)


def _dsa_kernel(q_ref, iq_ref, iw_ref, k_ref, v_ref, ik_ref, o_ref,
                keys_ref, bias_ref, m_ref, l_ref, acc_ref, run_ref, need_ref,
                *, tq, tk, n_sel):
    i = pl.program_id(1)
    nck = (i * tq) // tk + 1
    lo, half_b = _half_masks(tq)
    int_min = jnp.int32(INT_MIN)

    iq = iq_ref[...]
    iw = iw_ref[...]
    iqz = [iq[:, (hd // 2) * LANES:(hd // 2 + 1) * LANES] * half_b[hd % 2]
           for hd in range(N_IDX_HEADS)]
    wcol = [iw[:, hd:hd + 1] for hd in range(N_IDX_HEADS)]
    q_pos = i * tq + lax.broadcasted_iota(I32, (tq, tk), 0)
    k_off = lax.broadcasted_iota(I32, (tq, tk), 1)

    def score_chunk(c, carry):
        ikc = ik_ref[pl.ds(pl.multiple_of(c * tk, tk), tk), :]
        score = None
        for hd in range(N_IDX_HEADS):
            d = lax.dot_general(iqz[hd], ikc, NT_DIMS, preferred_element_type=F32)
            t = jnp.maximum(d, 0.0) * wcol[hd]
            score = t if score is None else score + t
        bits = pltpu.bitcast(score, I32)
        bits = jnp.where(bits == int_min, 0, bits)
        key = bits ^ ((bits >> 31) & 0x7FFFFFFF)
        keys_ref[c] = jnp.where(c * tk + k_off <= q_pos, key, int_min)
        return carry

    lax.fori_loop(0, nck, score_chunk, 0)

    def count_ge(cand):
        cand_b = jnp.broadcast_to(cand, (tq, LANES))

        def cnt_chunk(c, acc):
            kc = keys_ref[c]
            for jj in range(tk // LANES):
                acc = acc + jnp.where(kc[:, jj * LANES:(jj + 1) * LANES] >= cand_b, 1.0, 0.0)
            return acc

        acc = lax.fori_loop(0, nck, cnt_chunk, jnp.zeros((tq, LANES), F32))
        return jnp.sum(acc, axis=-1, keepdims=True)

    def bit_step(it, carry):
        t_u, cnt_t = carry
        cand_u = t_u | lax.shift_left(jnp.int32(1), 31 - it)
        cnt = count_ge(cand_u ^ int_min)
        ok = cnt >= n_sel
        return jnp.where(ok, cand_u, t_u), jnp.where(ok, cnt, cnt_t)

    t_u, cnt_t = lax.fori_loop(
        0, 32, bit_step,
        (jnp.zeros((tq, 1), I32), jnp.full((tq, 1), 1.0, F32) * (nck * tk).astype(F32)))
    thr = t_u ^ int_min
    exact = jnp.max(jnp.abs(cnt_t - n_sel)) == 0.0

    @pl.when(jnp.logical_not(exact))
    def _():
        need_ref[...] = n_sel - count_ge(thr + 1)
        run_ref[...] = jnp.zeros_like(run_ref)

    q = q_ref[...]
    qz = [q[:, (hd // 2) * LANES:(hd // 2 + 1) * LANES] * half_b[hd % 2]
          for hd in range(N_ATTN_HEADS)]
    m_ref[...] = jnp.full_like(m_ref, NEG_BIG)
    l_ref[...] = jnp.zeros_like(l_ref)
    acc_ref[...] = jnp.zeros_like(acc_ref)
    thr_sel = jnp.broadcast_to(jnp.maximum(thr, int_min + 1), (tq, tk))

    def att_chunk(c, carry):
        kc = keys_ref[c]

        @pl.when(exact)
        def _():
            bias_ref[...] = jnp.where(kc >= thr_sel, 0.0, NEG_BIG)

        @pl.when(jnp.logical_not(exact))
        def _():
            eq = kc == thr
            tri = (lax.broadcasted_iota(I32, (tk, tk), 0)
                   <= lax.broadcasted_iota(I32, (tk, tk), 1))
            prefix = jnp.dot(jnp.where(eq, 1.0, 0.0).astype(BF16),
                             jnp.where(tri, 1.0, 0.0).astype(BF16),
                             preferred_element_type=F32) + run_ref[...]
            take = ((kc > thr) | (eq & (prefix <= need_ref[...]))) & (kc > int_min)
            bias_ref[...] = jnp.where(take, 0.0, NEG_BIG)
            run_ref[...] += jnp.sum(jnp.where(eq, 1.0, 0.0), axis=-1, keepdims=True)

        start = pl.multiple_of(c * tk, tk)
        for hd in range(N_ATTN_HEADS):
            cols = slice((hd // 2) * LANES, (hd // 2 + 1) * LANES)
            s = lax.dot_general(qz[hd], k_ref[pl.ds(start, tk), cols], NT_DIMS,
                                preferred_element_type=F32) + bias_ref[...]
            m_old = m_ref[hd]
            m_new = jnp.maximum(m_old, jnp.max(s, axis=-1, keepdims=True))
            alpha = jnp.exp(m_old - m_new)
            p = jnp.exp(s - m_new)
            l_ref[hd] = alpha * l_ref[hd] + jnp.sum(p, axis=-1, keepdims=True)
            acc_ref[hd] = alpha * acc_ref[hd] + jnp.dot(
                p.astype(BF16), v_ref[pl.ds(start, tk), cols], preferred_element_type=F32)
            m_ref[hd] = m_new
        return carry

    lax.fori_loop(0, nck, att_chunk, 0)

    for pair in range(N_ATTN_HEADS // 2):
        even = acc_ref[2 * pair] / l_ref[2 * pair]
        odd = acc_ref[2 * pair + 1] / l_ref[2 * pair + 1]
        o_ref[:, pair * LANES:(pair + 1) * LANES] = jnp.where(lo, even, odd).astype(BF16)


def _dsa_call(q, k, v, iq, ik, iw, *, tq, tk):
    b, s, _ = q.shape
    n_sel = min(TOPK_MAX, s // 4)
    tile = lambda w: pl.BlockSpec((None, tq, w), lambda bi, i: (bi, i, 0))
    whole = lambda w: pl.BlockSpec((None, s, w), lambda bi, i: (bi, 0, 0),
                                   pipeline_mode=pl.Buffered(1))
    return pl.pallas_call(
        functools.partial(_dsa_kernel, tq=tq, tk=tk, n_sel=float(n_sel)),
        grid=(b, s // tq),
        in_specs=[tile(ATTN_W), tile(IDX_W), tile(LANES),
                  whole(ATTN_W), whole(ATTN_W), whole(LANES)],
        out_specs=tile(ATTN_W),
        out_shape=jax.ShapeDtypeStruct((b, s, ATTN_W), BF16),
        scratch_shapes=[
            pltpu.VMEM((s // tk, tq, tk), I32),
            pltpu.VMEM((tq, tk), F32),
            pltpu.VMEM((N_ATTN_HEADS, tq, 1), F32),
            pltpu.VMEM((N_ATTN_HEADS, tq, 1), F32),
            pltpu.VMEM((N_ATTN_HEADS, tq, LANES), F32),
            pltpu.VMEM((tq, 1), F32),
            pltpu.VMEM((tq, 1), F32),
        ],
        compiler_params=pltpu.CompilerParams(
            dimension_semantics=("arbitrary", "arbitrary"), vmem_limit_bytes=VMEM_LIMIT_BYTES),
        name="dsa",
    )(q, iq, iw, k, v, ik)


def _prep_w_in(w_in):
    offs = [sum(IN_SIZES[:n + 1]) for n in range(len(IN_SIZES) - 1)]
    aq, ak, av, iq, ik, iw, pu, mq = jnp.split(w_in, offs, axis=-1)
    pad = jnp.zeros(w_in.shape[:-1] + (LANES - N_IDX_HEADS,), w_in.dtype)
    return jnp.concatenate([aq, ak, av, iq, ik, ik, pu, mq, iw, pad], axis=-1).astype(BF16)


def _prep_ffn(gate, up, down, fc):
    depth, d, f = gate.shape
    split_cols = lambda w: w.astype(BF16).reshape(depth, d, f // fc, fc).transpose(0, 2, 1, 3)
    return split_cols(gate), split_cols(up), down.astype(BF16).reshape(depth, f // fc, fc, d)


def _block_diag(w_pool):
    depth, g, c, _ = w_pool.shape
    eye = jnp.eye(g, dtype=w_pool.dtype)
    return jnp.einsum("lgcd,gh->lgchd", w_pool, eye).reshape(depth, g * c, g * c).astype(BF16)


def kernel(x, mem, ffn1_norm, ffn1_gate, ffn1_up, ffn1_down, mix_norm, mem_norm, w_in, q_norm,
           k_norm, w_pool, pool_scale, w_mem_kv, mq_norm, mk_norm, w_out, ffn2_norm, ffn2_gate,
           ffn2_up, ffn2_down):
    b, s, d = x.shape
    depth = w_in.shape[0]
    tm = min(512, s)
    tq = min(128, s)
    tk = min(512, s)
    assert s % tm == 0 and s % tk == 0 and tk % tq == 0

    vec = lambda a, reps=1: jnp.tile(a, (1, reps))[:, None, :]
    ffn1 = _prep_ffn(ffn1_gate, ffn1_up, ffn1_down, MXU_N)
    ffn2 = _prep_ffn(ffn2_gate, ffn2_up, ffn2_down, MXU_N)
    w_in_p = _prep_w_in(w_in)
    w_out_b = w_out.astype(BF16)
    w_pool_bd = _block_diag(w_pool)
    head = jnp.arange(ATTN_W) // HEAD_DIM
    bd = jnp.where(head[:, None] == head[None, :], 1.0 / HEAD_DIM, 0.0).astype(BF16)
    qg, kg = vec(q_norm, N_ATTN_HEADS), vec(k_norm, N_ATTN_HEADS)
    mqg, mkg = vec(mq_norm, N_MEM_HEADS), vec(mk_norm, N_MEM_HEADS)
    f1g, f2g, mixg, memg, ps = vec(ffn1_norm), vec(ffn2_norm), vec(mix_norm), vec(mem_norm), vec(pool_scale)

    mk, mv = _memkv_call(mem, memg, w_mem_kv.astype(BF16), mkg, bd)

    x2d = x.reshape(b * s, d)
    for l in range(depth):
        x2d = _ffn_call(x2d, l, f1g, *ffn1, tm=tm)
        q, k, v, iq, ik, iw, y_pm = _proj_call(
            x2d.reshape(b, s, d), l, mixg, w_in_p, qg, kg, mqg, bd, w_pool_bd, ps, mk, mv, tm=tm)
        y_attn = _dsa_call(q, k, v, iq, ik, iw, tq=tq, tk=tk)
        x2d = _ffn_call(x2d, l, f2g, *ffn2, tm=tm,
                        mix=(y_attn.reshape(b * s, ATTN_W), y_pm.reshape(b * s, POOL_W + MEM_W),
                             w_out_b))
    return x2d.reshape(b, s, d)
```

```python
import functools
import math

import jax
import jax.numpy as jnp
from jax import lax
from jax.experimental import pallas as pl
from jax.experimental.pallas import tpu as pltpu

F32 = jnp.float32
BF16 = jnp.bfloat16
I32 = jnp.int32
I16 = jnp.int16

EPS = 1e-6
HEAD_DIM = 64
N_ATTN_HEADS = 8
N_IDX_HEADS = 4
N_POOL_GROUPS = 4
N_MEM_HEADS = 4
POOL_WINDOWS = (2, 4, 8, 16)
TOPK_MAX = 256

ATTN_W = N_ATTN_HEADS * HEAD_DIM
IDX_W = N_IDX_HEADS * HEAD_DIM
POOL_W = N_POOL_GROUPS * HEAD_DIM
MEM_W = N_MEM_HEADS * HEAD_DIM
IN_SIZES = (ATTN_W, ATTN_W, ATTN_W, IDX_W, HEAD_DIM, N_IDX_HEADS, POOL_W, MEM_W)

LANES = 128
SUBLANES = 8
PACK = 16
MXU_N = 256
VMEM_LIMIT_BYTES = 60 * 1024 * 1024
INT_MIN = -(2 ** 31)
I16_MIN = -(2 ** 15)
I16_MAX = 2 ** 15 - 1
NEG_BIG = -1e30
LOG2E = math.log2(math.e)
NORM_SLACK = 8.1
SAFE_LOG2_BOUND = 80.0

OFF_Q, OFF_K, OFF_V = 0, ATTN_W, 2 * ATTN_W
OFF_IQ = 3 * ATTN_W
OFF_IK = OFF_IQ + IDX_W
OFF_PU = OFF_IK + LANES
OFF_MQ = OFF_PU + POOL_W
OFF_IW = OFF_MQ + MEM_W
D_IN_PAD = OFF_IW + LANES

NT_DIMS = (((1,), (1,)), ((), ()))


def _rms(x, g):
    return x * lax.rsqrt(jnp.mean(x * x, axis=-1, keepdims=True) + EPS) * g


def _head_rms(v, gain, bd):
    sq = v * v
    hi = sq.astype(BF16)
    lo = (sq - hi.astype(F32)).astype(BF16)
    ms = (jnp.dot(hi, bd, preferred_element_type=F32)
          + jnp.dot(lo, bd, preferred_element_type=F32))
    return v * lax.rsqrt(ms + EPS) * gain


def _half_masks(rows):
    lane = lax.broadcasted_iota(I32, (rows, LANES), 1)
    lo = lane < HEAD_DIM
    lo_b = jnp.where(lo, 1.0, 0.0).astype(BF16)
    hi_b = jnp.where(lo, 0.0, 1.0).astype(BF16)
    return lo, (lo_b, hi_b)


def _ffn_kernel(*refs, mixed, n_chunks):
    if mixed:
        (x_ref, ya_ref, yb_ref, wo_ref, g_ref, wg_ref, wu_ref, wd_ref,
         o_ref, h_ref, acc_ref) = refs
        x = (x_ref[...]
             + jnp.dot(ya_ref[...], wo_ref[:ATTN_W, :], preferred_element_type=F32)
             + jnp.dot(yb_ref[...], wo_ref[ATTN_W:, :], preferred_element_type=F32))
    else:
        x_ref, g_ref, wg_ref, wu_ref, wd_ref, o_ref, h_ref, acc_ref = refs
        x = x_ref[...]
    o_ref[...] = x
    h_ref[...] = _rms(x, g_ref[...]).astype(BF16)
    acc_ref[...] = jnp.zeros_like(acc_ref)

    def chunk(c, carry):
        h = h_ref[...]
        g = jnp.dot(h, wg_ref[c], preferred_element_type=F32)
        u = jnp.dot(h, wu_ref[c], preferred_element_type=F32)
        a = (g * (1.0 / (1.0 + jnp.exp(-g))) * u).astype(BF16)
        acc_ref[...] += jnp.dot(a, wd_ref[c], preferred_element_type=F32)
        return carry

    lax.fori_loop(0, n_chunks, chunk, 0)
    o_ref[...] += 0.5 * acc_ref[...]


def _ffn_call(x2d, layer, gain, wg, wu, wd, mix=None, *, tm):
    n, d = x2d.shape
    n_chunks, _, fc = wg.shape[1:]
    const = lambda *shape: pl.BlockSpec((None,) + shape, lambda i: (layer,) + (0,) * len(shape),
                                        pipeline_mode=pl.Buffered(1))
    row = lambda w: pl.BlockSpec((tm, w), lambda i: (i, 0))
    in_specs, args = [row(d)], [x2d]
    if mix is not None:
        ya, yb, wo = mix
        in_specs += [row(ya.shape[1]), row(yb.shape[1]), const(*wo.shape[1:])]
        args += [ya, yb, wo]
    in_specs += [const(1, d), const(n_chunks, d, fc), const(n_chunks, d, fc), const(n_chunks, fc, d)]
    args += [gain, wg, wu, wd]
    return pl.pallas_call(
        functools.partial(_ffn_kernel, mixed=mix is not None, n_chunks=n_chunks),
        grid=(n // tm,),
        in_specs=in_specs,
        out_specs=row(d),
        out_shape=jax.ShapeDtypeStruct((n, d), F32),
        scratch_shapes=[pltpu.VMEM((tm, d), BF16), pltpu.VMEM((tm, d), F32)],
        compiler_params=pltpu.CompilerParams(
            dimension_semantics=("arbitrary",), vmem_limit_bytes=VMEM_LIMIT_BYTES),
        name="ffn_mixed" if mix is not None else "ffn",
    )(*args)


def _memkv_kernel(mem_ref, g_ref, w_ref, kg_ref, bd_ref, mk_ref, mv_ref):
    mem_n = _rms(mem_ref[...], g_ref[...]).astype(BF16)
    kv = jnp.dot(mem_n, w_ref[...], preferred_element_type=F32)
    mk_ref[...] = _head_rms(kv[:, :MEM_W], kg_ref[...], bd_ref[:MEM_W, :MEM_W]).astype(BF16)
    mv_ref[...] = kv[:, MEM_W:].astype(BF16)


def _memkv_call(mem, mem_norm, w_kv, mk_gain, bd):
    b, m, d = mem.shape
    depth = w_kv.shape[0]
    out = jax.ShapeDtypeStruct((depth, b, m, MEM_W), BF16)
    out_spec = pl.BlockSpec((None, None, m, MEM_W), lambda l, i: (l, i, 0, 0))
    return pl.pallas_call(
        _memkv_kernel,
        grid=(depth, b),
        in_specs=[
            pl.BlockSpec((None, m, d), lambda l, i: (i, 0, 0)),
            pl.BlockSpec((None, 1, d), lambda l, i: (l, 0, 0)),
            pl.BlockSpec((None, d, 2 * MEM_W), lambda l, i: (l, 0, 0)),
            pl.BlockSpec((None, 1, MEM_W), lambda l, i: (l, 0, 0)),
            pl.BlockSpec(bd.shape, lambda l, i: (0, 0)),
        ],
        out_specs=[out_spec, out_spec],
        out_shape=[out, out],
        compiler_params=pltpu.CompilerParams(dimension_semantics=("arbitrary", "arbitrary")),
        name="memkv",
    )(mem, mem_norm, w_kv, mk_gain, bd)


POOL_HALO = 32


def _proj_kernel(x_ref, g_ref, w_ref, qg_ref, kg_ref, mqg_ref, bd_ref, wp_ref, ps_ref,
                 mk_ref, mv_ref,
                 q_ref, k_ref, vt_ref, iq_ref, ik_ref, iwt_ref, y_ref,
                 p_ref, s2_ref, s4_ref, s8_ref, *, tm, tkv):
    j = pl.program_id(1)
    rows = tm + POOL_HALO
    h = _rms(x_ref[...], g_ref[...]).astype(BF16)
    z = jnp.dot(h, w_ref[...], preferred_element_type=F32)
    bd = bd_ref[...]

    q_ref[...] = (_head_rms(z[:, OFF_Q:OFF_Q + ATTN_W], qg_ref[...], bd)
                  * (HEAD_DIM ** -0.5 * LOG2E)).astype(BF16)
    k_ref[...] = _head_rms(z[:, OFF_K:OFF_K + ATTN_W], kg_ref[...], bd).astype(BF16)
    vt = z[:, OFF_V:OFF_V + ATTN_W].T
    for part in range(tm // tkv):
        vt_ref[part] = vt[:, part * tkv:(part + 1) * tkv].astype(BF16)
    iq_ref[...] = (z[:, OFF_IQ:OFF_IQ + IDX_W] * HEAD_DIM ** -0.5).astype(BF16)
    ik_ref[...] = z[:, OFF_IK:OFF_IK + LANES].astype(BF16)
    iwt_ref[...] = (z[:, OFF_IW:OFF_IW + LANES] * N_IDX_HEADS ** -0.5).T[0:SUBLANES, :]

    @pl.when(j == 0)
    def _():
        p_ref[0:POOL_HALO, :] = jnp.zeros((POOL_HALO, POOL_W), F32)

    pu = z[:, OFF_PU:OFF_PU + POOL_W]
    p_ref[POOL_HALO:rows, :] = pu
    s2_ref[8:rows, :] = p_ref[8:rows, :] + p_ref[7:rows - 1, :]
    s4_ref[16:rows, :] = s2_ref[16:rows, :] + s2_ref[14:rows - 2, :]
    s8_ref[24:rows, :] = s4_ref[24:rows, :] + s4_ref[20:rows - 4, :]
    s16 = s8_ref[POOL_HALO:rows, :] + s8_ref[24:rows - 8, :]
    lane = lax.broadcasted_iota(I32, (tm, POOL_W), 1)
    g1, g2, g3 = lane < HEAD_DIM, lane < 2 * HEAD_DIM, lane < 3 * HEAD_DIM
    wsum = jnp.where(g1, s2_ref[POOL_HALO:rows, :],
                     jnp.where(g2, s4_ref[POOL_HALO:rows, :],
                               jnp.where(g3, s8_ref[POOL_HALO:rows, :], s16)))
    win = jnp.where(g1, float(POOL_WINDOWS[0]),
                    jnp.where(g2, float(POOL_WINDOWS[1]),
                              jnp.where(g3, float(POOL_WINDOWS[2]), float(POOL_WINDOWS[3]))))
    t1 = (j * tm + 1 + lax.broadcasted_iota(I32, (tm, POOL_W), 0)).astype(F32)
    pooled = (wsum / jnp.minimum(t1, win) - pu).astype(BF16)
    y_pool = jnp.dot(pooled, wp_ref[...], preferred_element_type=F32) * ps_ref[...]
    y_ref[:, 0:POOL_W] = y_pool.astype(BF16)
    p_ref[16:POOL_HALO, :] = p_ref[tm + 16:rows, :]

    lo, half_b = _half_masks(tm)
    mqn = (_head_rms(z[:, OFF_MQ:OFF_MQ + MEM_W], mqg_ref[...], bd[:MEM_W, :MEM_W])
           * HEAD_DIM ** -0.5).astype(BF16)
    mk = mk_ref[...]
    mv = mv_ref[...]
    for pair in range(N_MEM_HEADS // 2):
        cols = slice(pair * LANES, (pair + 1) * LANES)
        outs = []
        for sub in range(2):
            logits = lax.dot_general(mqn[:, cols] * half_b[sub], mk[:, cols], NT_DIMS,
                                     preferred_element_type=F32)
            e = jnp.exp(logits - jnp.max(logits, axis=-1, keepdims=True))
            p = (e / jnp.sum(e, axis=-1, keepdims=True)).astype(BF16)
            outs.append(jnp.dot(p, mv[:, cols], preferred_element_type=F32))
        y_ref[:, POOL_W + pair * LANES:POOL_W + (pair + 1) * LANES] = (
            jnp.where(lo, outs[0], outs[1]).astype(BF16))


def _proj_call(x, layer, gain, w_in, qg, kg, mqg, bd, wp, ps, mk, mv, *, tm, tkv):
    b, s, d = x.shape
    m = mk.shape[2]
    lay = lambda *shape: pl.BlockSpec((None,) + shape, lambda i, j: (layer,) + (0,) * len(shape),
                                      pipeline_mode=pl.Buffered(1))
    tile = lambda w: pl.BlockSpec((None, tm, w), lambda i, j: (i, j, 0))
    mem = pl.BlockSpec((None, None, m, MEM_W), lambda i, j: (layer, i, 0, 0))
    sds = jax.ShapeDtypeStruct
    out_shape = [sds((b, s, ATTN_W), BF16), sds((b, s, ATTN_W), BF16),
                 sds((b, s // tkv, ATTN_W, tkv), BF16),
                 sds((b, s, IDX_W), BF16), sds((b, s, LANES), BF16),
                 sds((b, SUBLANES, s), F32),
                 sds((b, s, POOL_W + MEM_W), BF16)]
    out_specs = [tile(ATTN_W), tile(ATTN_W),
                 pl.BlockSpec((None, tm // tkv, ATTN_W, tkv), lambda i, j: (i, j, 0, 0)),
                 tile(IDX_W), tile(LANES),
                 pl.BlockSpec((None, SUBLANES, tm), lambda i, j: (i, 0, j)),
                 tile(POOL_W + MEM_W)]
    rows = tm + POOL_HALO
    return pl.pallas_call(
        functools.partial(_proj_kernel, tm=tm, tkv=tkv),
        grid=(b, s // tm),
        in_specs=[tile(d), lay(1, d), lay(d, D_IN_PAD), lay(1, ATTN_W), lay(1, ATTN_W),
                  lay(1, MEM_W), pl.BlockSpec(bd.shape, lambda i, j: (0, 0)),
                  lay(POOL_W, POOL_W), lay(1, POOL_W), mem, mem],
        out_specs=out_specs,
        out_shape=out_shape,
        scratch_shapes=[pltpu.VMEM((rows, POOL_W), F32) for _ in range(4)],
        compiler_params=pltpu.CompilerParams(
            dimension_semantics=("arbitrary", "arbitrary"), vmem_limit_bytes=VMEM_LIMIT_BYTES),
        name="proj",
    )(x, gain, w_in, qg, kg, mqg, bd, wp, ps, mk, mv)


def _dsa_kernel(flag_ref, q_ref, iq_ref, iw_ref, k_ref, vt_ref, ik_ref, o_ref,
                hi_ref, lo_ref, bias_ref, s_ref, p_ref, m_ref, l_ref, acc_ref, run_ref, need_ref,
                *, tq, tk, tkv, n_sel):
    i = pl.program_id(1)
    nck = (i * tq) // tk + 1
    _, half_b = _half_masks(tq)
    q_pos = i * tq + lax.broadcasted_iota(I32, (tk, tq), 1)
    k_off = lax.broadcasted_iota(I32, (tk, tq), 0)

    iq = iq_ref[...]
    iw = iw_ref[...]
    iqz = [iq[:, (hd // 2) * LANES:(hd // 2 + 1) * LANES] * half_b[hd % 2]
           for hd in range(N_IDX_HEADS)]
    wrow = [iw[hd:hd + 1, :] for hd in range(N_IDX_HEADS)]

    def score_chunk(c, carry):
        ikc = ik_ref[pl.ds(pl.multiple_of(c * tk, tk), tk), :]
        score = None
        for hd in range(N_IDX_HEADS):
            d = lax.dot_general(ikc, iqz[hd], NT_DIMS, preferred_element_type=F32)
            t = jnp.maximum(d, 0.0) * wrow[hd]
            score = t if score is None else score + t
        bits = pltpu.bitcast(score, I32)
        bits = jnp.where(bits == INT_MIN, 0, bits)
        key = bits ^ ((bits >> 31) & 0x7FFFFFFF)
        key = jnp.where(c * tk + k_off <= q_pos, key, INT_MIN)
        hi_ref[c] = (key >> 16).astype(I16)
        lo_ref[c] = ((key & 0xFFFF) + I16_MIN).astype(I16)
        return carry

    lax.fori_loop(0, nck, score_chunk, 0)

    def count_ge(ref, cand):
        cand_b = jnp.broadcast_to(cand, (PACK, tq)).astype(I16)
        one = jnp.ones((PACK, tq), I16)
        zero = jnp.zeros((PACK, tq), I16)

        def cnt_chunk(c, acc):
            a = ref[c]
            for r in range(tk // PACK):
                acc = acc + jnp.where(a[r * PACK:(r + 1) * PACK, :] >= cand_b, one, zero)
            return acc

        acc = lax.fori_loop(0, nck, cnt_chunk, zero)
        return jnp.sum(acc.astype(I32).astype(F32), axis=0, keepdims=True)

    def greedy(ref, target, cnt0):
        def step(it, carry):
            t_u, cnt_t = carry
            cand_u = t_u | lax.shift_left(jnp.int32(1), 15 - it)
            cnt = count_ge(ref, cand_u + I16_MIN)
            ok = cnt >= target
            return jnp.where(ok, cand_u, t_u), jnp.where(ok, cnt, cnt_t)
        t_u, cnt_t = lax.fori_loop(0, 16, step, (jnp.zeros((1, tq), I32), cnt0))
        return t_u + I16_MIN, cnt_t

    total = jnp.full((1, tq), 1.0, F32) * (nck * tk).astype(F32)
    h_thr, cnt_ge_h = greedy(hi_ref, n_sel, total)
    cnt_gt_h = jnp.where(h_thr == I16_MAX, 0.0,
                         count_ge(hi_ref, jnp.minimum(h_thr + 1, I16_MAX)))
    h_b = jnp.broadcast_to(h_thr, (PACK, tq)).astype(I16)

    def mask_chunk(c, carry):
        hi = hi_ref[c]
        lo = lo_ref[c]
        for r in range(tk // PACK):
            rows = slice(r * PACK, (r + 1) * PACK)
            lo_ref[c, rows, :] = jnp.where(hi[rows, :] == h_b, lo[rows, :],
                                           jnp.full((PACK, tq), I16_MIN, I16))
        return carry

    lax.fori_loop(0, nck, mask_chunk, 0)
    l_thr, cnt_ge_l = greedy(lo_ref, n_sel - cnt_gt_h, cnt_ge_h - cnt_gt_h)
    n_ge = cnt_gt_h + cnt_ge_l
    bad = jnp.maximum(jnp.abs(n_ge - n_sel), jnp.where(h_thr == I16_MIN, 1.0, 0.0))
    exact = jnp.max(bad) == 0.0

    @pl.when(jnp.logical_not(exact))
    def _():
        cnt_gt_l = jnp.where(l_thr == I16_MAX, 0.0,
                             count_ge(lo_ref, jnp.minimum(l_thr + 1, I16_MAX)))
        need_ref[...] = n_sel - cnt_gt_h - cnt_gt_l
        run_ref[...] = jnp.zeros_like(run_ref)

    q = q_ref[...]
    qz = [q[:, (hd // 2) * LANES:(hd // 2 + 1) * LANES] * half_b[hd % 2]
          for hd in range(N_ATTN_HEADS)]
    l_b = jnp.broadcast_to(l_thr, (PACK, tq)).astype(I16)
    k_off_v = lax.broadcasted_iota(I32, (tkv, tq), 0)
    q_pos_v = i * tq + lax.broadcasted_iota(I32, (tkv, tq), 1)

    def make_bias(c):
        sub = pl.ds(pl.multiple_of((c % (tk // tkv)) * tkv, tkv), tkv)
        hi = hi_ref[c // (tk // tkv), sub, :]
        lo = lo_ref[c // (tk // tkv), sub, :]

        @pl.when(exact)
        def _():
            for r in range(tkv // PACK):
                rows = slice(r * PACK, (r + 1) * PACK)
                sel = (hi[rows, :] > h_b) | ((hi[rows, :] == h_b) & (lo[rows, :] >= l_b))
                bias_ref[rows, :] = jnp.where(sel, jnp.zeros((PACK, tq), BF16),
                                              jnp.full((PACK, tq), NEG_BIG, BF16)).astype(F32)

        @pl.when(jnp.logical_not(exact))
        def _():
            hi32 = hi.astype(I32)
            lo32 = lo.astype(I32)
            is_h = hi32 == h_thr
            eq = is_h & (lo32 == l_thr)
            gt = (hi32 > h_thr) | (is_h & (lo32 > l_thr))
            tri = (lax.broadcasted_iota(I32, (tkv, tkv), 1)
                   <= lax.broadcasted_iota(I32, (tkv, tkv), 0))
            prefix = jnp.dot(jnp.where(tri, 1.0, 0.0).astype(BF16),
                             jnp.where(eq, 1.0, 0.0).astype(BF16),
                             preferred_element_type=F32) + run_ref[...]
            take = (gt | (eq & (prefix <= need_ref[...]))) & (c * tkv + k_off_v <= q_pos_v)
            bias_ref[...] = jnp.where(take, 0.0, NEG_BIG)
            run_ref[...] += jnp.sum(jnp.where(eq, 1.0, 0.0), axis=0, keepdims=True)

    def attention(online):
        if online:
            m_ref[...] = jnp.full_like(m_ref, NEG_BIG)
        l_ref[...] = jnp.zeros_like(l_ref)
        acc_ref[...] = jnp.zeros_like(acc_ref)

        def att_chunk(c, carry):
            make_bias(c)
            start = pl.multiple_of(c * tkv, tkv)
            for hd in range(N_ATTN_HEADS):
                cols = slice((hd // 2) * LANES, (hd // 2 + 1) * LANES)
                s_ref[hd] = lax.dot_general(k_ref[pl.ds(start, tkv), cols], qz[hd], NT_DIMS,
                                            preferred_element_type=F32)
            for hd in range(N_ATTN_HEADS):
                s = s_ref[hd] + bias_ref[...]
                if online:
                    m_old = m_ref[hd]
                    m_new = jnp.maximum(m_old, jnp.max(s, axis=0, keepdims=True))
                    alpha = jnp.exp2(m_old - m_new)
                    p = jnp.exp2(s - m_new)
                    l_ref[hd] = alpha * l_ref[hd] + jnp.sum(p, axis=0, keepdims=True)
                    acc_ref[hd] = alpha * acc_ref[hd]
                    m_ref[hd] = m_new
                else:
                    p = jnp.exp2(s)
                    l_ref[hd] += jnp.sum(p, axis=0, keepdims=True)
                p_ref[hd] = p.astype(BF16)
            for hd in range(N_ATTN_HEADS):
                acc_ref[hd] += jnp.dot(vt_ref[c, hd * HEAD_DIM:(hd + 1) * HEAD_DIM, :], p_ref[hd],
                                       preferred_element_type=F32)
            return carry

        lax.fori_loop(0, (i * tq) // tkv + 1, att_chunk, 0)

    bounded = flag_ref[0] != 0

    @pl.when(bounded)
    def _():
        attention(False)

    @pl.when(jnp.logical_not(bounded))
    def _():
        attention(True)

    for pair in range(N_ATTN_HEADS // 2):
        both = jnp.concatenate([acc_ref[2 * pair] / l_ref[2 * pair],
                                acc_ref[2 * pair + 1] / l_ref[2 * pair + 1]], axis=0)
        o_ref[:, pair * LANES:(pair + 1) * LANES] = both.T.astype(BF16)


def _dsa_call(flag, q, k, vt, iq, ik, iwt, *, tq, tk):
    b, s, _ = q.shape
    tkv = vt.shape[-1]
    n_sel = min(TOPK_MAX, s // 4)
    tile = lambda w: pl.BlockSpec((None, tq, w), lambda bi, i, f: (bi, i, 0))
    whole = lambda *shape: pl.BlockSpec((None,) + shape, lambda bi, i, f: (bi,) + (0,) * len(shape),
                                        pipeline_mode=pl.Buffered(1))
    grid_spec = pltpu.PrefetchScalarGridSpec(
        num_scalar_prefetch=1,
        grid=(b, s // tq),
        in_specs=[tile(ATTN_W), tile(IDX_W),
                  pl.BlockSpec((None, SUBLANES, tq), lambda bi, i, f: (bi, 0, i)),
                  whole(s, ATTN_W), whole(s // tkv, ATTN_W, tkv), whole(s, LANES)],
        out_specs=tile(ATTN_W),
        scratch_shapes=[
            pltpu.VMEM((s // tk, tk, tq), I16),
            pltpu.VMEM((s // tk, tk, tq), I16),
            pltpu.VMEM((tkv, tq), F32),
            pltpu.VMEM((N_ATTN_HEADS, tkv, tq), F32),
            pltpu.VMEM((N_ATTN_HEADS, tkv, tq), BF16),
            pltpu.VMEM((N_ATTN_HEADS, 1, tq), F32),
            pltpu.VMEM((N_ATTN_HEADS, 1, tq), F32),
            pltpu.VMEM((N_ATTN_HEADS, HEAD_DIM, tq), F32),
            pltpu.VMEM((1, tq), F32),
            pltpu.VMEM((1, tq), F32),
        ])
    return pl.pallas_call(
        functools.partial(_dsa_kernel, tq=tq, tk=tk, tkv=tkv, n_sel=float(n_sel)),
        grid_spec=grid_spec,
        out_shape=jax.ShapeDtypeStruct((b, s, ATTN_W), BF16),
        compiler_params=pltpu.CompilerParams(
            dimension_semantics=("arbitrary", "arbitrary"), vmem_limit_bytes=VMEM_LIMIT_BYTES),
        name="dsa",
    )(flag, q, iq, iwt, k, vt, ik)


def _prep_w_in(w_in):
    offs = [sum(IN_SIZES[:n + 1]) for n in range(len(IN_SIZES) - 1)]
    aq, ak, av, iq, ik, iw, pu, mq = jnp.split(w_in, offs, axis=-1)
    pad = jnp.zeros(w_in.shape[:-1] + (LANES - N_IDX_HEADS,), w_in.dtype)
    return jnp.concatenate([aq, ak, av, iq, ik, ik, pu, mq, iw, pad], axis=-1).astype(BF16)


def _prep_ffn(gate, up, down, fc):
    depth, d, f = gate.shape
    split_cols = lambda w: w.astype(BF16).reshape(depth, d, f // fc, fc).transpose(0, 2, 1, 3)
    return split_cols(gate), split_cols(up), down.astype(BF16).reshape(depth, f // fc, fc, d)


def _block_diag(w_pool):
    depth, g, c, _ = w_pool.shape
    eye = jnp.eye(g, dtype=w_pool.dtype)
    return jnp.einsum("lgcd,gh->lgchd", w_pool, eye).reshape(depth, g * c, g * c).astype(BF16)


def kernel(x, mem, ffn1_norm, ffn1_gate, ffn1_up, ffn1_down, mix_norm, mem_norm, w_in, q_norm,
           k_norm, w_pool, pool_scale, w_mem_kv, mq_norm, mk_norm, w_out, ffn2_norm, ffn2_gate,
           ffn2_up, ffn2_down):
    b, s, d = x.shape
    depth = w_in.shape[0]
    tm = min(512, s)
    tq = min(256, s)
    tk = min(512, s)
    tkv = min(256, s)
    assert s % tm == 0 and s % tk == 0 and tk % tq == 0 and tk % tkv == 0 and tm % tkv == 0

    vec = lambda a, reps=1: jnp.tile(a, (1, reps))[:, None, :]
    ffn1 = _prep_ffn(ffn1_gate, ffn1_up, ffn1_down, MXU_N)
    ffn2 = _prep_ffn(ffn2_gate, ffn2_up, ffn2_down, MXU_N)
    w_in_p = _prep_w_in(w_in)
    w_out_b = w_out.astype(BF16)
    w_pool_bd = _block_diag(w_pool)
    head = jnp.arange(ATTN_W) // HEAD_DIM
    bd = jnp.where(head[:, None] == head[None, :], 1.0 / HEAD_DIM, 0.0).astype(BF16)
    qg, kg = vec(q_norm, N_ATTN_HEADS), vec(k_norm, N_ATTN_HEADS)
    mqg, mkg = vec(mq_norm, N_MEM_HEADS), vec(mk_norm, N_MEM_HEADS)
    f1g, f2g, mixg, memg, ps = vec(ffn1_norm), vec(ffn2_norm), vec(mix_norm), vec(mem_norm), vec(pool_scale)
    log2_bound = (NORM_SLACK * LOG2E * jnp.max(jnp.abs(q_norm), axis=-1)
                  * jnp.max(jnp.abs(k_norm), axis=-1))
    bounded = (log2_bound <= SAFE_LOG2_BOUND).astype(I32)

    mk, mv = _memkv_call(mem, memg, w_mem_kv.astype(BF16), mkg, bd)

    x2d = x.reshape(b * s, d)
    for l in range(depth):
        x2d = _ffn_call(x2d, l, f1g, *ffn1, tm=tm)
        q, k, vt, iq, ik, iwt, y_pm = _proj_call(
            x2d.reshape(b, s, d), l, mixg, w_in_p, qg, kg, mqg, bd, w_pool_bd, ps, mk, mv,
            tm=tm, tkv=tkv)
        y_attn = _dsa_call(bounded[l:l + 1], q, k, vt, iq, ik, iwt, tq=tq, tk=tk)
        x2d = _ffn_call(x2d, l, f2g, *ffn2, tm=tm,
                        mix=(y_attn.reshape(b * s, ATTN_W), y_pm.reshape(b * s, POOL_W + MEM_W),
                             w_out_b))
    return x2d.reshape(b, s, d)
```

```python
import functools
import math

import jax
import jax.numpy as jnp
from jax import lax
from jax.experimental import pallas as pl
from jax.experimental.pallas import tpu as pltpu

F32 = jnp.float32
BF16 = jnp.bfloat16
I32 = jnp.int32
I16 = jnp.int16

EPS = 1e-6
HEAD_DIM = 64
N_ATTN_HEADS = 8
N_IDX_HEADS = 4
N_POOL_GROUPS = 4
N_MEM_HEADS = 4
POOL_WINDOWS = (2, 4, 8, 16)
TOPK_MAX = 256

ATTN_W = N_ATTN_HEADS * HEAD_DIM
IDX_W = N_IDX_HEADS * HEAD_DIM
POOL_W = N_POOL_GROUPS * HEAD_DIM
MEM_W = N_MEM_HEADS * HEAD_DIM
IN_SIZES = (ATTN_W, ATTN_W, ATTN_W, IDX_W, HEAD_DIM, N_IDX_HEADS, POOL_W, MEM_W)

LANES = 128
SUBLANES = 8
PACK = 16
CNT_ROWS = 4 * PACK
MXU_N = 256
VMEM_LIMIT_BYTES = 60 * 1024 * 1024
INT_MIN = -(2 ** 31)
I16_MIN = -(2 ** 15)
I16_MAX = 2 ** 15 - 1
NEG_BIG = -1e30
LOG2E = math.log2(math.e)
NORM_SLACK = 8.1
SAFE_LOG2_BOUND = 80.0

OFF_Q, OFF_K, OFF_V = 0, ATTN_W, 2 * ATTN_W
OFF_IQ = 3 * ATTN_W
OFF_IK = OFF_IQ + IDX_W
OFF_PU = OFF_IK + LANES
OFF_MQ = OFF_PU + POOL_W
OFF_IW = OFF_MQ + MEM_W
D_IN_PAD = OFF_IW + LANES

NT_DIMS = (((1,), (1,)), ((), ()))


def _rms(x, g):
    return x * lax.rsqrt(jnp.mean(x * x, axis=-1, keepdims=True) + EPS) * g


def _head_rms(v, gain, bd):
    sq = v * v
    hi = sq.astype(BF16)
    lo = (sq - hi.astype(F32)).astype(BF16)
    ms = (jnp.dot(hi, bd, preferred_element_type=F32)
          + jnp.dot(lo, bd, preferred_element_type=F32))
    return v * lax.rsqrt(ms + EPS) * gain


def _half_masks(rows):
    lane = lax.broadcasted_iota(I32, (rows, LANES), 1)
    lo = lane < HEAD_DIM
    lo_b = jnp.where(lo, 1.0, 0.0).astype(BF16)
    hi_b = jnp.where(lo, 0.0, 1.0).astype(BF16)
    return lo, (lo_b, hi_b)


def _ffn_kernel(*refs, mixed, n_chunks):
    if mixed:
        (x_ref, ya_ref, yb_ref, wo_ref, g_ref, wg_ref, wu_ref, wd_ref,
         o_ref, h_ref, acc_ref) = refs
        x = (x_ref[...]
             + jnp.dot(ya_ref[...], wo_ref[:ATTN_W, :], preferred_element_type=F32)
             + jnp.dot(yb_ref[...], wo_ref[ATTN_W:, :], preferred_element_type=F32))
    else:
        x_ref, g_ref, wg_ref, wu_ref, wd_ref, o_ref, h_ref, acc_ref = refs
        x = x_ref[...]
    o_ref[...] = x
    h_ref[...] = _rms(x, g_ref[...]).astype(BF16)
    acc_ref[...] = jnp.zeros_like(acc_ref)

    def chunk(c, carry):
        h = h_ref[...]
        g = jnp.dot(h, wg_ref[c], preferred_element_type=F32)
        u = jnp.dot(h, wu_ref[c], preferred_element_type=F32)
        a = (g * (1.0 / (1.0 + jnp.exp(-g))) * u).astype(BF16)
        acc_ref[...] += jnp.dot(a, wd_ref[c], preferred_element_type=F32)
        return carry

    lax.fori_loop(0, n_chunks, chunk, 0)
    o_ref[...] += 0.5 * acc_ref[...]


def _ffn_call(x2d, layer, gain, wg, wu, wd, mix=None, *, tm):
    n, d = x2d.shape
    n_chunks, _, fc = wg.shape[1:]
    const = lambda *shape: pl.BlockSpec((None,) + shape, lambda i: (layer,) + (0,) * len(shape),
                                        pipeline_mode=pl.Buffered(1))
    row = lambda w: pl.BlockSpec((tm, w), lambda i: (i, 0))
    in_specs, args = [row(d)], [x2d]
    if mix is not None:
        ya, yb, wo = mix
        in_specs += [row(ya.shape[1]), row(yb.shape[1]), const(*wo.shape[1:])]
        args += [ya, yb, wo]
    in_specs += [const(1, d), const(n_chunks, d, fc), const(n_chunks, d, fc), const(n_chunks, fc, d)]
    args += [gain, wg, wu, wd]
    return pl.pallas_call(
        functools.partial(_ffn_kernel, mixed=mix is not None, n_chunks=n_chunks),
        grid=(n // tm,),
        in_specs=in_specs,
        out_specs=row(d),
        out_shape=jax.ShapeDtypeStruct((n, d), F32),
        scratch_shapes=[pltpu.VMEM((tm, d), BF16), pltpu.VMEM((tm, d), F32)],
        compiler_params=pltpu.CompilerParams(
            dimension_semantics=("arbitrary",), vmem_limit_bytes=VMEM_LIMIT_BYTES),
        name="ffn_mixed" if mix is not None else "ffn",
    )(*args)


def _memkv_kernel(mem_ref, g_ref, w_ref, kg_ref, bd_ref, mk_ref, mv_ref):
    mem_n = _rms(mem_ref[...], g_ref[...]).astype(BF16)
    kv = jnp.dot(mem_n, w_ref[...], preferred_element_type=F32)
    mk_ref[...] = _head_rms(kv[:, :MEM_W], kg_ref[...], bd_ref[:MEM_W, :MEM_W]).astype(BF16)
    mv_ref[...] = kv[:, MEM_W:].astype(BF16)


def _memkv_call(mem, mem_norm, w_kv, mk_gain, bd):
    b, m, d = mem.shape
    depth = w_kv.shape[0]
    out = jax.ShapeDtypeStruct((depth, b, m, MEM_W), BF16)
    out_spec = pl.BlockSpec((None, None, m, MEM_W), lambda l, i: (l, i, 0, 0))
    return pl.pallas_call(
        _memkv_kernel,
        grid=(depth, b),
        in_specs=[
            pl.BlockSpec((None, m, d), lambda l, i: (i, 0, 0)),
            pl.BlockSpec((None, 1, d), lambda l, i: (l, 0, 0)),
            pl.BlockSpec((None, d, 2 * MEM_W), lambda l, i: (l, 0, 0)),
            pl.BlockSpec((None, 1, MEM_W), lambda l, i: (l, 0, 0)),
            pl.BlockSpec(bd.shape, lambda l, i: (0, 0)),
        ],
        out_specs=[out_spec, out_spec],
        out_shape=[out, out],
        compiler_params=pltpu.CompilerParams(dimension_semantics=("arbitrary", "arbitrary")),
        name="memkv",
    )(mem, mem_norm, w_kv, mk_gain, bd)


POOL_HALO = 32


def _proj_kernel(x_ref, g_ref, w_ref, qg_ref, kg_ref, mqg_ref, bd_ref, wp_ref, ps_ref,
                 mk_ref, mv_ref,
                 q_ref, k_ref, vt_ref, iq_ref, ik_ref, iwt_ref, y_ref,
                 p_ref, s2_ref, s4_ref, s8_ref, *, tm, tkv):
    j = pl.program_id(1)
    rows = tm + POOL_HALO
    h = _rms(x_ref[...], g_ref[...]).astype(BF16)
    z = jnp.dot(h, w_ref[...], preferred_element_type=F32)
    bd = bd_ref[...]

    q_ref[...] = (_head_rms(z[:, OFF_Q:OFF_Q + ATTN_W], qg_ref[...], bd)
                  * (HEAD_DIM ** -0.5 * LOG2E)).T.astype(BF16)
    k_ref[...] = _head_rms(z[:, OFF_K:OFF_K + ATTN_W], kg_ref[...], bd).astype(BF16)
    vt = z[:, OFF_V:OFF_V + ATTN_W].T
    for part in range(tm // tkv):
        vt_ref[part] = vt[:, part * tkv:(part + 1) * tkv].astype(BF16)
    iq_ref[...] = (z[:, OFF_IQ:OFF_IQ + IDX_W] * HEAD_DIM ** -0.5).T.astype(BF16)
    ik_ref[...] = z[:, OFF_IK:OFF_IK + LANES].astype(BF16)
    iwt_ref[...] = (z[:, OFF_IW:OFF_IW + LANES] * N_IDX_HEADS ** -0.5).T[0:SUBLANES, :]

    @pl.when(j == 0)
    def _():
        p_ref[0:POOL_HALO, :] = jnp.zeros((POOL_HALO, POOL_W), F32)

    pu = z[:, OFF_PU:OFF_PU + POOL_W]
    p_ref[POOL_HALO:rows, :] = pu
    s2_ref[8:rows, :] = p_ref[8:rows, :] + p_ref[7:rows - 1, :]
    s4_ref[16:rows, :] = s2_ref[16:rows, :] + s2_ref[14:rows - 2, :]
    s8_ref[24:rows, :] = s4_ref[24:rows, :] + s4_ref[20:rows - 4, :]
    s16 = s8_ref[POOL_HALO:rows, :] + s8_ref[24:rows - 8, :]
    lane = lax.broadcasted_iota(I32, (tm, POOL_W), 1)
    g1, g2, g3 = lane < HEAD_DIM, lane < 2 * HEAD_DIM, lane < 3 * HEAD_DIM
    wsum = jnp.where(g1, s2_ref[POOL_HALO:rows, :],
                     jnp.where(g2, s4_ref[POOL_HALO:rows, :],
                               jnp.where(g3, s8_ref[POOL_HALO:rows, :], s16)))
    win = jnp.where(g1, float(POOL_WINDOWS[0]),
                    jnp.where(g2, float(POOL_WINDOWS[1]),
                              jnp.where(g3, float(POOL_WINDOWS[2]), float(POOL_WINDOWS[3]))))
    t1 = (j * tm + 1 + lax.broadcasted_iota(I32, (tm, POOL_W), 0)).astype(F32)
    pooled = (wsum / jnp.minimum(t1, win) - pu).astype(BF16)
    y_pool = jnp.dot(pooled, wp_ref[...], preferred_element_type=F32) * ps_ref[...]
    y_ref[:, 0:POOL_W] = y_pool.astype(BF16)
    p_ref[16:POOL_HALO, :] = p_ref[tm + 16:rows, :]

    lo, half_b = _half_masks(tm)
    mqn = (_head_rms(z[:, OFF_MQ:OFF_MQ + MEM_W], mqg_ref[...], bd[:MEM_W, :MEM_W])
           * HEAD_DIM ** -0.5).astype(BF16)
    mk = mk_ref[...]
    mv = mv_ref[...]
    for pair in range(N_MEM_HEADS // 2):
        cols = slice(pair * LANES, (pair + 1) * LANES)
        outs = []
        for sub in range(2):
            logits = lax.dot_general(mqn[:, cols] * half_b[sub], mk[:, cols], NT_DIMS,
                                     preferred_element_type=F32)
            e = jnp.exp(logits - jnp.max(logits, axis=-1, keepdims=True))
            p = (e / jnp.sum(e, axis=-1, keepdims=True)).astype(BF16)
            outs.append(jnp.dot(p, mv[:, cols], preferred_element_type=F32))
        y_ref[:, POOL_W + pair * LANES:POOL_W + (pair + 1) * LANES] = (
            jnp.where(lo, outs[0], outs[1]).astype(BF16))


def _proj_call(x, layer, gain, w_in, qg, kg, mqg, bd, wp, ps, mk, mv, *, tm, tkv):
    b, s, d = x.shape
    m = mk.shape[2]
    lay = lambda *shape: pl.BlockSpec((None,) + shape, lambda i, j: (layer,) + (0,) * len(shape),
                                      pipeline_mode=pl.Buffered(1))
    tile = lambda w: pl.BlockSpec((None, tm, w), lambda i, j: (i, j, 0))
    mem = pl.BlockSpec((None, None, m, MEM_W), lambda i, j: (layer, i, 0, 0))
    sds = jax.ShapeDtypeStruct
    tile_t = lambda w: pl.BlockSpec((None, w, tm), lambda i, j: (i, 0, j))
    out_shape = [sds((b, ATTN_W, s), BF16),
                 sds((b, s, ATTN_W), BF16),
                 sds((b, s // tkv, ATTN_W, tkv), BF16),
                 sds((b, IDX_W, s), BF16),
                 sds((b, s, LANES), BF16),
                 sds((b, SUBLANES, s), F32),
                 sds((b, s, POOL_W + MEM_W), BF16)]
    out_specs = [tile_t(ATTN_W), tile(ATTN_W),
                 pl.BlockSpec((None, tm // tkv, ATTN_W, tkv), lambda i, j: (i, j, 0, 0)),
                 tile_t(IDX_W), tile(LANES), tile_t(SUBLANES),
                 tile(POOL_W + MEM_W)]
    rows = tm + POOL_HALO
    return pl.pallas_call(
        functools.partial(_proj_kernel, tm=tm, tkv=tkv),
        grid=(b, s // tm),
        in_specs=[tile(d), lay(1, d), lay(d, D_IN_PAD), lay(1, ATTN_W), lay(1, ATTN_W),
                  lay(1, MEM_W), pl.BlockSpec(bd.shape, lambda i, j: (0, 0)),
                  lay(POOL_W, POOL_W), lay(1, POOL_W), mem, mem],
        out_specs=out_specs,
        out_shape=out_shape,
        scratch_shapes=[pltpu.VMEM((rows, POOL_W), F32) for _ in range(4)],
        compiler_params=pltpu.CompilerParams(
            dimension_semantics=("arbitrary", "arbitrary"), vmem_limit_bytes=VMEM_LIMIT_BYTES),
        name="proj",
    )(x, gain, w_in, qg, kg, mqg, bd, wp, ps, mk, mv)


def _dsa_kernel(flag_ref, q_ref, iq_ref, iw_ref, k_ref, vt_ref, ik_ref, o_ref,
                hi_ref, lo_ref, bias_ref, p_ref, m_ref, l_ref, acc_ref, run_ref, need_ref,
                *, tq, tk, tkv, n_sel):
    i = pl.program_id(1)
    nck = (i * tq) // tk + 1
    top = lax.broadcasted_iota(I32, (LANES, tq), 0) < HEAD_DIM
    half_b = (jnp.where(top, 1.0, 0.0).astype(BF16), jnp.where(top, 0.0, 1.0).astype(BF16))
    q_pos = i * tq + lax.broadcasted_iota(I32, (tk, tq), 1)
    k_off = lax.broadcasted_iota(I32, (tk, tq), 0)

    iq = iq_ref[...]
    iw = iw_ref[...]
    iqz = [iq[(hd // 2) * LANES:(hd // 2 + 1) * LANES, :] * half_b[hd % 2]
           for hd in range(N_IDX_HEADS)]
    wrow = [iw[hd:hd + 1, :] for hd in range(N_IDX_HEADS)]

    def score_chunk(c, causal):
        ikc = ik_ref[pl.ds(pl.multiple_of(c * tk, tk), tk), :]
        score = None
        for hd in range(N_IDX_HEADS):
            d = jnp.dot(ikc, iqz[hd], preferred_element_type=F32)
            t = jnp.maximum(d, 0.0) * wrow[hd]
            score = t if score is None else score + t
        bits = pltpu.bitcast(score, I32)
        neg = bits >> 31
        key = (bits ^ (neg & 0x7FFFFFFF)) - neg
        if causal:
            key = jnp.where(c * tk + k_off <= q_pos, key, INT_MIN)
        hi_ref[c] = (key >> 16).astype(I16)
        lo_ref[c] = (key ^ 0x8000).astype(I16)
        return 0

    lax.fori_loop(0, nck - 1, lambda c, _: score_chunk(c, False), 0)
    score_chunk(nck - 1, True)

    def count_ge(ref, cand):
        cand_b = jnp.broadcast_to(cand, (CNT_ROWS, tq)).astype(I16)
        one = jnp.ones((CNT_ROWS, tq), I16)
        zero = jnp.zeros((CNT_ROWS, tq), I16)

        def cnt_chunk(c, acc):
            a = ref[c]
            for r in range(tk // CNT_ROWS):
                acc = acc + jnp.where(a[r * CNT_ROWS:(r + 1) * CNT_ROWS, :] >= cand_b, one, zero)
            return acc

        acc = lax.fori_loop(0, nck, cnt_chunk, zero)
        return jnp.sum(acc.astype(I32).astype(F32), axis=0, keepdims=True)

    def greedy(ref, target, cnt0):
        def step(it, carry):
            t_u, cnt_t = carry
            cand_u = t_u | lax.shift_left(jnp.int32(1), 15 - it)
            cnt = count_ge(ref, cand_u + I16_MIN)
            ok = cnt >= target
            return jnp.where(ok, cand_u, t_u), jnp.where(ok, cnt, cnt_t)
        t_u, cnt_t = lax.fori_loop(0, 16, step, (jnp.zeros((1, tq), I32), cnt0))
        return t_u + I16_MIN, cnt_t

    total = jnp.full((1, tq), 1.0, F32) * (nck * tk).astype(F32)
    h_thr, cnt_ge_h = greedy(hi_ref, n_sel, total)
    cnt_gt_h = jnp.where(h_thr == I16_MAX, 0.0,
                         count_ge(hi_ref, jnp.minimum(h_thr + 1, I16_MAX)))
    h_b = jnp.broadcast_to(h_thr, (PACK, tq)).astype(I16)

    def mask_chunk(c, carry):
        hi = hi_ref[c]
        lo = lo_ref[c]
        for r in range(tk // PACK):
            rows = slice(r * PACK, (r + 1) * PACK)
            lo_ref[c, rows, :] = jnp.where(hi[rows, :] == h_b, lo[rows, :],
                                           jnp.full((PACK, tq), I16_MIN, I16))
        return carry

    lax.fori_loop(0, nck, mask_chunk, 0)
    l_thr, cnt_ge_l = greedy(lo_ref, n_sel - cnt_gt_h, cnt_ge_h - cnt_gt_h)
    n_ge = cnt_gt_h + cnt_ge_l
    bad = jnp.maximum(jnp.abs(n_ge - n_sel), jnp.where(h_thr == I16_MIN, 1.0, 0.0))
    exact = jnp.max(bad) == 0.0

    @pl.when(jnp.logical_not(exact))
    def _():
        cnt_gt_l = jnp.where(l_thr == I16_MAX, 0.0,
                             count_ge(lo_ref, jnp.minimum(l_thr + 1, I16_MAX)))
        need_ref[...] = n_sel - cnt_gt_h - cnt_gt_l
        run_ref[...] = jnp.zeros_like(run_ref)

    q = q_ref[...]
    qz = [q[(hd // 2) * LANES:(hd // 2 + 1) * LANES, :] * half_b[hd % 2]
          for hd in range(N_ATTN_HEADS)]
    take_all = l_thr == I16_MIN
    h_sel = jnp.broadcast_to(jnp.where(take_all, h_thr - 1, h_thr), (PACK, tq)).astype(I16)
    l_sel = jnp.broadcast_to(jnp.where(take_all, I16_MIN + 1, l_thr), (PACK, tq)).astype(I16)
    k_off_v = lax.broadcasted_iota(I32, (tkv, tq), 0)
    q_pos_v = i * tq + lax.broadcasted_iota(I32, (tkv, tq), 1)

    def make_bias(c):
        sub = pl.ds(pl.multiple_of((c % (tk // tkv)) * tkv, tkv), tkv)
        hi = hi_ref[c // (tk // tkv), sub, :]
        lo = lo_ref[c // (tk // tkv), sub, :]

        @pl.when(exact)
        def _():
            for r in range(tkv // PACK):
                rows = slice(r * PACK, (r + 1) * PACK)
                sel = (hi[rows, :] > h_sel) | (lo[rows, :] >= l_sel)
                bias_ref[rows, :] = jnp.where(sel, jnp.zeros((PACK, tq), BF16),
                                              jnp.full((PACK, tq), NEG_BIG, BF16)).astype(F32)

        @pl.when(jnp.logical_not(exact))
        def _():
            hi32 = hi.astype(I32)
            lo32 = lo.astype(I32)
            is_h = hi32 == h_thr
            eq = is_h & (lo32 == l_thr)
            gt = (hi32 > h_thr) | (is_h & (lo32 > l_thr))
            tri = (lax.broadcasted_iota(I32, (tkv, tkv), 1)
                   <= lax.broadcasted_iota(I32, (tkv, tkv), 0))
            prefix = jnp.dot(jnp.where(tri, 1.0, 0.0).astype(BF16),
                             jnp.where(eq, 1.0, 0.0).astype(BF16),
                             preferred_element_type=F32) + run_ref[...]
            take = (gt | (eq & (prefix <= need_ref[...]))) & (c * tkv + k_off_v <= q_pos_v)
            bias_ref[...] = jnp.where(take, 0.0, NEG_BIG)
            run_ref[...] += jnp.sum(jnp.where(eq, 1.0, 0.0), axis=0, keepdims=True)

    def attention(online):
        if online:
            m_ref[...] = jnp.full_like(m_ref, NEG_BIG)
        l_ref[...] = jnp.zeros_like(l_ref)
        acc_ref[...] = jnp.zeros_like(acc_ref)

        def att_chunk(c, carry):
            make_bias(c)
            start = pl.multiple_of(c * tkv, tkv)
            for hd in range(N_ATTN_HEADS):
                cols = slice((hd // 2) * LANES, (hd // 2 + 1) * LANES)
                s = jnp.dot(k_ref[pl.ds(start, tkv), cols], qz[hd],
                            preferred_element_type=F32) + bias_ref[...]
                if online:
                    m_old = m_ref[hd]
                    m_new = jnp.maximum(m_old, jnp.max(s, axis=0, keepdims=True))
                    alpha = jnp.exp2(m_old - m_new)
                    p = jnp.exp2(s - m_new)
                    l_ref[hd] = alpha * l_ref[hd] + jnp.sum(p, axis=0, keepdims=True)
                    acc_ref[hd] = alpha * acc_ref[hd]
                    m_ref[hd] = m_new
                else:
                    p = jnp.exp2(s)
                    l_ref[hd] += jnp.sum(p, axis=0, keepdims=True)
                p_ref[hd] = p.astype(BF16)
            for hd in range(N_ATTN_HEADS):
                acc_ref[hd] += jnp.dot(vt_ref[c, hd * HEAD_DIM:(hd + 1) * HEAD_DIM, :], p_ref[hd],
                                       preferred_element_type=F32)
            return carry

        lax.fori_loop(0, (i * tq) // tkv + 1, att_chunk, 0)

    bounded = flag_ref[0] != 0

    @pl.when(bounded)
    def _():
        attention(False)

    @pl.when(jnp.logical_not(bounded))
    def _():
        attention(True)

    for pair in range(N_ATTN_HEADS // 2):
        both = jnp.concatenate([acc_ref[2 * pair] / l_ref[2 * pair],
                                acc_ref[2 * pair + 1] / l_ref[2 * pair + 1]], axis=0)
        o_ref[:, pair * LANES:(pair + 1) * LANES] = both.T.astype(BF16)


def _dsa_call(flag, q, k, vt, iq, ik, iwt, *, tq, tk):
    b, s, _ = k.shape
    tkv = vt.shape[-1]
    n_sel = min(TOPK_MAX, s // 4)
    tile_t = lambda w: pl.BlockSpec((None, w, tq), lambda bi, i, f: (bi, 0, i))
    whole = lambda *shape: pl.BlockSpec((None,) + shape, lambda bi, i, f: (bi,) + (0,) * len(shape),
                                        pipeline_mode=pl.Buffered(1))
    grid_spec = pltpu.PrefetchScalarGridSpec(
        num_scalar_prefetch=1,
        grid=(b, s // tq),
        in_specs=[tile_t(ATTN_W), tile_t(IDX_W), tile_t(SUBLANES),
                  whole(s, ATTN_W), whole(s // tkv, ATTN_W, tkv), whole(s, LANES)],
        out_specs=pl.BlockSpec((None, tq, ATTN_W), lambda bi, i, f: (bi, i, 0)),
        scratch_shapes=[
            pltpu.VMEM((s // tk, tk, tq), I16),
            pltpu.VMEM((s // tk, tk, tq), I16),
            pltpu.VMEM((tkv, tq), F32),
            pltpu.VMEM((N_ATTN_HEADS, tkv, tq), BF16),
            pltpu.VMEM((N_ATTN_HEADS, 1, tq), F32),
            pltpu.VMEM((N_ATTN_HEADS, 1, tq), F32),
            pltpu.VMEM((N_ATTN_HEADS, HEAD_DIM, tq), F32),
            pltpu.VMEM((1, tq), F32),
            pltpu.VMEM((1, tq), F32),
        ])
    return pl.pallas_call(
        functools.partial(_dsa_kernel, tq=tq, tk=tk, tkv=tkv, n_sel=float(n_sel)),
        grid_spec=grid_spec,
        out_shape=jax.ShapeDtypeStruct((b, s, ATTN_W), BF16),
        compiler_params=pltpu.CompilerParams(
            dimension_semantics=("arbitrary", "arbitrary"), vmem_limit_bytes=VMEM_LIMIT_BYTES),
        name="dsa",
    )(flag, q, iq, iwt, k, vt, ik)


def _prep_w_in(w_in):
    offs = [sum(IN_SIZES[:n + 1]) for n in range(len(IN_SIZES) - 1)]
    aq, ak, av, iq, ik, iw, pu, mq = jnp.split(w_in, offs, axis=-1)
    pad = jnp.zeros(w_in.shape[:-1] + (LANES - N_IDX_HEADS,), w_in.dtype)
    return jnp.concatenate([aq, ak, av, iq, ik, ik, pu, mq, iw, pad], axis=-1).astype(BF16)


def _prep_ffn(gate, up, down, fc):
    depth, d, f = gate.shape
    split_cols = lambda w: w.astype(BF16).reshape(depth, d, f // fc, fc).transpose(0, 2, 1, 3)
    return split_cols(gate), split_cols(up), down.astype(BF16).reshape(depth, f // fc, fc, d)


def _block_diag(w_pool):
    depth, g, c, _ = w_pool.shape
    eye = jnp.eye(g, dtype=w_pool.dtype)
    return jnp.einsum("lgcd,gh->lgchd", w_pool, eye).reshape(depth, g * c, g * c).astype(BF16)


def kernel(x, mem, ffn1_norm, ffn1_gate, ffn1_up, ffn1_down, mix_norm, mem_norm, w_in, q_norm,
           k_norm, w_pool, pool_scale, w_mem_kv, mq_norm, mk_norm, w_out, ffn2_norm, ffn2_gate,
           ffn2_up, ffn2_down):
    b, s, d = x.shape
    depth = w_in.shape[0]
    tm = min(512, s)
    tq = min(256, s)
    tk = min(512, s)
    tkv = min(256, s)
    assert s % tm == 0 and s % tk == 0 and tk % tq == 0 and tk % tkv == 0 and tm % tkv == 0

    vec = lambda a, reps=1: jnp.tile(a, (1, reps))[:, None, :]
    ffn1 = _prep_ffn(ffn1_gate, ffn1_up, ffn1_down, MXU_N)
    ffn2 = _prep_ffn(ffn2_gate, ffn2_up, ffn2_down, MXU_N)
    w_in_p = _prep_w_in(w_in)
    w_out_b = w_out.astype(BF16)
    w_pool_bd = _block_diag(w_pool)
    head = jnp.arange(ATTN_W) // HEAD_DIM
    bd = jnp.where(head[:, None] == head[None, :], 1.0 / HEAD_DIM, 0.0).astype(BF16)
    qg, kg = vec(q_norm, N_ATTN_HEADS), vec(k_norm, N_ATTN_HEADS)
    mqg, mkg = vec(mq_norm, N_MEM_HEADS), vec(mk_norm, N_MEM_HEADS)
    f1g, f2g, mixg, memg, ps = vec(ffn1_norm), vec(ffn2_norm), vec(mix_norm), vec(mem_norm), vec(pool_scale)
    log2_bound = (NORM_SLACK * LOG2E * jnp.max(jnp.abs(q_norm), axis=-1)
                  * jnp.max(jnp.abs(k_norm), axis=-1))
    bounded = (log2_bound <= SAFE_LOG2_BOUND).astype(I32)

    mk, mv = _memkv_call(mem, memg, w_mem_kv.astype(BF16), mkg, bd)

    x2d = x.reshape(b * s, d)
    for l in range(depth):
        x2d = _ffn_call(x2d, l, f1g, *ffn1, tm=tm)
        q, k, vt, iq, ik, iwt, y_pm = _proj_call(
            x2d.reshape(b, s, d), l, mixg, w_in_p, qg, kg, mqg, bd, w_pool_bd, ps, mk, mv,
            tm=tm, tkv=tkv)
        y_attn = _dsa_call(bounded[l:l + 1], q, k, vt, iq, ik, iwt, tq=tq, tk=tk)
        x2d = _ffn_call(x2d, l, f2g, *ffn2, tm=tm,
                        mix=(y_attn.reshape(b * s, ATTN_W), y_pm.reshape(b * s, POOL_W + MEM_W),
                             w_out_b))
    return x2d.reshape(b, s, d)
```

```python
import functools
import math

import jax
import jax.numpy as jnp
from jax import lax
from jax.experimental import pallas as pl
from jax.experimental.pallas import tpu as pltpu

F32 = jnp.float32
BF16 = jnp.bfloat16
I32 = jnp.int32
I16 = jnp.int16

EPS = 1e-6
HEAD_DIM = 64
N_ATTN_HEADS = 8
N_IDX_HEADS = 4
N_POOL_GROUPS = 4
N_MEM_HEADS = 4
POOL_WINDOWS = (2, 4, 8, 16)
TOPK_MAX = 256

ATTN_W = N_ATTN_HEADS * HEAD_DIM
IDX_W = N_IDX_HEADS * HEAD_DIM
POOL_W = N_POOL_GROUPS * HEAD_DIM
MEM_W = N_MEM_HEADS * HEAD_DIM
IN_SIZES = (ATTN_W, ATTN_W, ATTN_W, IDX_W, HEAD_DIM, N_IDX_HEADS, POOL_W, MEM_W)

LANES = 128
SUBLANES = 8
PACK = 16
CNT_ROWS = 4 * PACK
MXU_N = 256
VMEM_LIMIT_BYTES = 60 * 1024 * 1024
INT_MIN = -(2 ** 31)
I16_MIN = -(2 ** 15)
I16_MAX = 2 ** 15 - 1
NEG_BIG = -1e30
LOG2E = math.log2(math.e)
NORM_SLACK = 8.1
SAFE_LOG2_BOUND = 80.0

OFF_Q, OFF_K, OFF_V = 0, ATTN_W, 2 * ATTN_W
OFF_IQ = 3 * ATTN_W
OFF_IK = OFF_IQ + IDX_W
OFF_PU = OFF_IK + LANES
OFF_MQ = OFF_PU + POOL_W
OFF_IW = OFF_MQ + MEM_W
D_IN_PAD = OFF_IW + LANES

NT_DIMS = (((1,), (1,)), ((), ()))


def _rms(x, g):
    return x * lax.rsqrt(jnp.mean(x * x, axis=-1, keepdims=True) + EPS) * g


def _head_rms(v, gain, bd):
    sq = v * v
    hi = sq.astype(BF16)
    lo = (sq - hi.astype(F32)).astype(BF16)
    ms = (jnp.dot(hi, bd, preferred_element_type=F32)
          + jnp.dot(lo, bd, preferred_element_type=F32))
    return v * lax.rsqrt(ms + EPS) * gain


def _half_masks(rows):
    lane = lax.broadcasted_iota(I32, (rows, LANES), 1)
    lo = lane < HEAD_DIM
    lo_b = jnp.where(lo, 1.0, 0.0).astype(BF16)
    hi_b = jnp.where(lo, 0.0, 1.0).astype(BF16)
    return lo, (lo_b, hi_b)


def _ffn_kernel(*refs, mixed, n_chunks):
    if mixed:
        (x_ref, ya_ref, yb_ref, wo_ref, g_ref, wg_ref, wu_ref, wd_ref,
         o_ref, h_ref, acc_ref) = refs
        x = (x_ref[...]
             + jnp.dot(ya_ref[...], wo_ref[:ATTN_W, :], preferred_element_type=F32)
             + jnp.dot(yb_ref[...], wo_ref[ATTN_W:, :], preferred_element_type=F32))
    else:
        x_ref, g_ref, wg_ref, wu_ref, wd_ref, o_ref, h_ref, acc_ref = refs
        x = x_ref[...]
    o_ref[...] = x
    h_ref[...] = _rms(x, g_ref[...]).astype(BF16)
    acc_ref[...] = jnp.zeros_like(acc_ref)

    def chunk(c, carry):
        h = h_ref[...]
        g = jnp.dot(h, wg_ref[c], preferred_element_type=F32)
        u = jnp.dot(h, wu_ref[c], preferred_element_type=F32)
        a = (g * (1.0 / (1.0 + jnp.exp(-g))) * u).astype(BF16)
        acc_ref[...] += jnp.dot(a, wd_ref[c], preferred_element_type=F32)
        return carry

    lax.fori_loop(0, n_chunks, chunk, 0, unroll=True)
    o_ref[...] += 0.5 * acc_ref[...]


def _ffn_call(x2d, layer, gain, wg, wu, wd, mix=None, *, tm):
    n, d = x2d.shape
    n_chunks, _, fc = wg.shape[1:]
    const = lambda *shape: pl.BlockSpec((None,) + shape, lambda i: (layer,) + (0,) * len(shape),
                                        pipeline_mode=pl.Buffered(1))
    row = lambda w: pl.BlockSpec((tm, w), lambda i: (i, 0))
    in_specs, args = [row(d)], [x2d]
    if mix is not None:
        ya, yb, wo = mix
        in_specs += [row(ya.shape[1]), row(yb.shape[1]), const(*wo.shape[1:])]
        args += [ya, yb, wo]
    in_specs += [const(1, d), const(n_chunks, d, fc), const(n_chunks, d, fc), const(n_chunks, fc, d)]
    args += [gain, wg, wu, wd]
    return pl.pallas_call(
        functools.partial(_ffn_kernel, mixed=mix is not None, n_chunks=n_chunks),
        grid=(n // tm,),
        in_specs=in_specs,
        out_specs=row(d),
        out_shape=jax.ShapeDtypeStruct((n, d), F32),
        scratch_shapes=[pltpu.VMEM((tm, d), BF16), pltpu.VMEM((tm, d), F32)],
        compiler_params=pltpu.CompilerParams(
            dimension_semantics=("arbitrary",), vmem_limit_bytes=VMEM_LIMIT_BYTES),
        name="ffn_mixed" if mix is not None else "ffn",
    )(*args)


def _memkv_kernel(mem_ref, g_ref, w_ref, kg_ref, bd_ref, mk_ref, mv_ref):
    mem_n = _rms(mem_ref[...], g_ref[...]).astype(BF16)
    kv = jnp.dot(mem_n, w_ref[...], preferred_element_type=F32)
    mk_ref[...] = _head_rms(kv[:, :MEM_W], kg_ref[...], bd_ref[:MEM_W, :MEM_W]).astype(BF16)
    mv_ref[...] = kv[:, MEM_W:].astype(BF16)


def _memkv_call(mem, mem_norm, w_kv, mk_gain, bd):
    b, m, d = mem.shape
    depth = w_kv.shape[0]
    out = jax.ShapeDtypeStruct((depth, b, m, MEM_W), BF16)
    out_spec = pl.BlockSpec((None, None, m, MEM_W), lambda l, i: (l, i, 0, 0))
    return pl.pallas_call(
        _memkv_kernel,
        grid=(depth, b),
        in_specs=[
            pl.BlockSpec((None, m, d), lambda l, i: (i, 0, 0)),
            pl.BlockSpec((None, 1, d), lambda l, i: (l, 0, 0)),
            pl.BlockSpec((None, d, 2 * MEM_W), lambda l, i: (l, 0, 0)),
            pl.BlockSpec((None, 1, MEM_W), lambda l, i: (l, 0, 0)),
            pl.BlockSpec(bd.shape, lambda l, i: (0, 0)),
        ],
        out_specs=[out_spec, out_spec],
        out_shape=[out, out],
        compiler_params=pltpu.CompilerParams(dimension_semantics=("arbitrary", "arbitrary")),
        name="memkv",
    )(mem, mem_norm, w_kv, mk_gain, bd)


POOL_HALO = 32


def _proj_kernel(x_ref, g_ref, w_ref, qg_ref, kg_ref, mqg_ref, bd_ref, wp_ref, ps_ref,
                 mk_ref, mv_ref,
                 q_ref, k_ref, vt_ref, iq_ref, ik_ref, iwt_ref, y_ref,
                 p_ref, s2_ref, s4_ref, s8_ref, *, tm, tkv):
    j = pl.program_id(1)
    rows = tm + POOL_HALO
    h = _rms(x_ref[...], g_ref[...]).astype(BF16)
    z = jnp.dot(h, w_ref[...], preferred_element_type=F32)
    bd = bd_ref[...]

    q_ref[...] = (_head_rms(z[:, OFF_Q:OFF_Q + ATTN_W], qg_ref[...], bd)
                  * (HEAD_DIM ** -0.5 * LOG2E)).T.astype(BF16)
    kn = _head_rms(z[:, OFF_K:OFF_K + ATTN_W], kg_ref[...], bd).astype(BF16)
    for pair in range(N_ATTN_HEADS // 2):
        k_ref[pair] = kn[:, pair * LANES:(pair + 1) * LANES]
    vt = z[:, OFF_V:OFF_V + ATTN_W].T
    for part in range(tm // tkv):
        vt_ref[part] = vt[:, part * tkv:(part + 1) * tkv].astype(BF16)
    iq_ref[...] = (z[:, OFF_IQ:OFF_IQ + IDX_W] * HEAD_DIM ** -0.5).T.astype(BF16)
    ik_ref[...] = z[:, OFF_IK:OFF_IK + LANES].astype(BF16)
    iwt_ref[...] = (z[:, OFF_IW:OFF_IW + LANES] * N_IDX_HEADS ** -0.5).T[0:SUBLANES, :]

    @pl.when(j == 0)
    def _():
        p_ref[0:POOL_HALO, :] = jnp.zeros((POOL_HALO, POOL_W), F32)

    pu = z[:, OFF_PU:OFF_PU + POOL_W]
    p_ref[POOL_HALO:rows, :] = pu
    s2_ref[8:rows, :] = p_ref[8:rows, :] + p_ref[7:rows - 1, :]
    s4_ref[16:rows, :] = s2_ref[16:rows, :] + s2_ref[14:rows - 2, :]
    s8_ref[24:rows, :] = s4_ref[24:rows, :] + s4_ref[20:rows - 4, :]
    s16 = s8_ref[POOL_HALO:rows, :] + s8_ref[24:rows - 8, :]
    lane = lax.broadcasted_iota(I32, (tm, POOL_W), 1)
    g1, g2, g3 = lane < HEAD_DIM, lane < 2 * HEAD_DIM, lane < 3 * HEAD_DIM
    wsum = jnp.where(g1, s2_ref[POOL_HALO:rows, :],
                     jnp.where(g2, s4_ref[POOL_HALO:rows, :],
                               jnp.where(g3, s8_ref[POOL_HALO:rows, :], s16)))
    win = jnp.where(g1, float(POOL_WINDOWS[0]),
                    jnp.where(g2, float(POOL_WINDOWS[1]),
                              jnp.where(g3, float(POOL_WINDOWS[2]), float(POOL_WINDOWS[3]))))
    t1 = (j * tm + 1 + lax.broadcasted_iota(I32, (tm, POOL_W), 0)).astype(F32)
    pooled = (wsum / jnp.minimum(t1, win) - pu).astype(BF16)
    y_pool = jnp.dot(pooled, wp_ref[...], preferred_element_type=F32) * ps_ref[...]
    y_ref[:, 0:POOL_W] = y_pool.astype(BF16)
    p_ref[16:POOL_HALO, :] = p_ref[tm + 16:rows, :]

    lo, half_b = _half_masks(tm)
    mqn = (_head_rms(z[:, OFF_MQ:OFF_MQ + MEM_W], mqg_ref[...], bd[:MEM_W, :MEM_W])
           * HEAD_DIM ** -0.5).astype(BF16)
    mk = mk_ref[...]
    mv = mv_ref[...]
    for pair in range(N_MEM_HEADS // 2):
        cols = slice(pair * LANES, (pair + 1) * LANES)
        outs = []
        for sub in range(2):
            logits = lax.dot_general(mqn[:, cols] * half_b[sub], mk[:, cols], NT_DIMS,
                                     preferred_element_type=F32)
            e = jnp.exp(logits - jnp.max(logits, axis=-1, keepdims=True))
            p = (e / jnp.sum(e, axis=-1, keepdims=True)).astype(BF16)
            outs.append(jnp.dot(p, mv[:, cols], preferred_element_type=F32))
        y_ref[:, POOL_W + pair * LANES:POOL_W + (pair + 1) * LANES] = (
            jnp.where(lo, outs[0], outs[1]).astype(BF16))


def _proj_call(x, layer, gain, w_in, qg, kg, mqg, bd, wp, ps, mk, mv, *, tm, tkv):
    b, s, d = x.shape
    m = mk.shape[2]
    lay = lambda *shape: pl.BlockSpec((None,) + shape, lambda i, j: (layer,) + (0,) * len(shape),
                                      pipeline_mode=pl.Buffered(1))
    tile = lambda w: pl.BlockSpec((None, tm, w), lambda i, j: (i, j, 0))
    mem = pl.BlockSpec((None, None, m, MEM_W), lambda i, j: (layer, i, 0, 0))
    sds = jax.ShapeDtypeStruct
    tile_t = lambda w: pl.BlockSpec((None, w, tm), lambda i, j: (i, 0, j))
    out_shape = [sds((b, ATTN_W, s), BF16),
                 sds((b, N_ATTN_HEADS // 2, s, LANES), BF16),
                 sds((b, s // tkv, ATTN_W, tkv), BF16),
                 sds((b, IDX_W, s), BF16),
                 sds((b, s, LANES), BF16),
                 sds((b, SUBLANES, s), F32),
                 sds((b, s, POOL_W + MEM_W), BF16)]
    out_specs = [tile_t(ATTN_W),
                 pl.BlockSpec((None, N_ATTN_HEADS // 2, tm, LANES), lambda i, j: (i, 0, j, 0)),
                 pl.BlockSpec((None, tm // tkv, ATTN_W, tkv), lambda i, j: (i, j, 0, 0)),
                 tile_t(IDX_W), tile(LANES), tile_t(SUBLANES),
                 tile(POOL_W + MEM_W)]
    rows = tm + POOL_HALO
    return pl.pallas_call(
        functools.partial(_proj_kernel, tm=tm, tkv=tkv),
        grid=(b, s // tm),
        in_specs=[tile(d), lay(1, d), lay(d, D_IN_PAD), lay(1, ATTN_W), lay(1, ATTN_W),
                  lay(1, MEM_W), pl.BlockSpec(bd.shape, lambda i, j: (0, 0)),
                  lay(POOL_W, POOL_W), lay(1, POOL_W), mem, mem],
        out_specs=out_specs,
        out_shape=out_shape,
        scratch_shapes=[pltpu.VMEM((rows, POOL_W), F32) for _ in range(4)],
        compiler_params=pltpu.CompilerParams(
            dimension_semantics=("arbitrary", "arbitrary"), vmem_limit_bytes=VMEM_LIMIT_BYTES),
        name="proj",
    )(x, gain, w_in, qg, kg, mqg, bd, wp, ps, mk, mv)


def _dsa_kernel(flag_ref, q_ref, iq_ref, iw_ref, k_ref, vt_ref, ik_ref, o_ref,
                hi_ref, lo_ref, bias_ref, p_ref, m_ref, l_ref, acc_ref, run_ref, need_ref,
                *, tq, tk, tkv, n_sel):
    i = pl.program_id(1)
    nck = (i * tq) // tk + 1
    top = lax.broadcasted_iota(I32, (LANES, tq), 0) < HEAD_DIM
    half_b = (jnp.where(top, 1.0, 0.0).astype(BF16), jnp.where(top, 0.0, 1.0).astype(BF16))
    q_pos = i * tq + lax.broadcasted_iota(I32, (tk, tq), 1)
    k_off = lax.broadcasted_iota(I32, (tk, tq), 0)

    iq = iq_ref[...]
    iw = iw_ref[...]
    iqz = [iq[(hd // 2) * LANES:(hd // 2 + 1) * LANES, :] * half_b[hd % 2]
           for hd in range(N_IDX_HEADS)]
    wrow = [iw[hd:hd + 1, :] for hd in range(N_IDX_HEADS)]

    def score_chunk(c, causal):
        ikc = ik_ref[pl.ds(pl.multiple_of(c * tk, tk), tk), :]
        score = None
        for hd in range(N_IDX_HEADS):
            d = jnp.dot(ikc, iqz[hd], preferred_element_type=F32)
            t = jnp.maximum(d, 0.0) * wrow[hd]
            score = t if score is None else score + t
        bits = pltpu.bitcast(score, I32)
        neg = bits >> 31
        key = (bits ^ (neg & 0x7FFFFFFF)) - neg
        if causal:
            key = jnp.where(c * tk + k_off <= q_pos, key, INT_MIN)
        hi_ref[c] = (key >> 16).astype(I16)
        lo_ref[c] = (key ^ 0x8000).astype(I16)
        return 0

    lax.fori_loop(0, nck - 1, lambda c, _: score_chunk(c, False), 0)
    score_chunk(nck - 1, True)

    def count_ge(ref, cand):
        cand_b = jnp.broadcast_to(cand, (CNT_ROWS, tq)).astype(I16)
        one = jnp.ones((CNT_ROWS, tq), I16)
        zero = jnp.zeros((CNT_ROWS, tq), I16)

        def cnt_chunk(c, acc):
            a = ref[c]
            for r in range(tk // CNT_ROWS):
                acc = acc + jnp.where(a[r * CNT_ROWS:(r + 1) * CNT_ROWS, :] >= cand_b, one, zero)
            return acc

        acc = lax.fori_loop(0, nck, cnt_chunk, zero)
        return jnp.sum(acc.astype(I32).astype(F32), axis=0, keepdims=True)

    def greedy(ref, target, cnt0):
        def step(it, carry):
            t_u, cnt_t = carry
            cand_u = t_u | lax.shift_left(jnp.int32(1), 15 - it)
            cnt = count_ge(ref, cand_u + I16_MIN)
            ok = cnt >= target
            return jnp.where(ok, cand_u, t_u), jnp.where(ok, cnt, cnt_t)
        t_u, cnt_t = lax.fori_loop(0, 16, step, (jnp.zeros((1, tq), I32), cnt0))
        return t_u + I16_MIN, cnt_t

    total = jnp.full((1, tq), 1.0, F32) * (nck * tk).astype(F32)
    h_thr, cnt_ge_h = greedy(hi_ref, n_sel, total)
    cnt_gt_h = jnp.where(h_thr == I16_MAX, 0.0,
                         count_ge(hi_ref, jnp.minimum(h_thr + 1, I16_MAX)))
    h_b = jnp.broadcast_to(h_thr, (PACK, tq)).astype(I16)

    def mask_chunk(c, carry):
        hi = hi_ref[c]
        lo = lo_ref[c]
        for r in range(tk // PACK):
            rows = slice(r * PACK, (r + 1) * PACK)
            lo_ref[c, rows, :] = jnp.where(hi[rows, :] == h_b, lo[rows, :],
                                           jnp.full((PACK, tq), I16_MIN, I16))
        return carry

    lax.fori_loop(0, nck, mask_chunk, 0)
    l_thr, cnt_ge_l = greedy(lo_ref, n_sel - cnt_gt_h, cnt_ge_h - cnt_gt_h)
    n_ge = cnt_gt_h + cnt_ge_l
    bad = jnp.maximum(jnp.abs(n_ge - n_sel), jnp.where(h_thr == I16_MIN, 1.0, 0.0))
    exact = jnp.max(bad) == 0.0

    @pl.when(jnp.logical_not(exact))
    def _():
        cnt_gt_l = jnp.where(l_thr == I16_MAX, 0.0,
                             count_ge(lo_ref, jnp.minimum(l_thr + 1, I16_MAX)))
        need_ref[...] = n_sel - cnt_gt_h - cnt_gt_l
        run_ref[...] = jnp.zeros_like(run_ref)

    q = q_ref[...]
    qz = [q[(hd // 2) * LANES:(hd // 2 + 1) * LANES, :] * half_b[hd % 2]
          for hd in range(N_ATTN_HEADS)]
    take_all = l_thr == I16_MIN
    h_sel = jnp.broadcast_to(jnp.where(take_all, h_thr - 1, h_thr), (PACK, tq)).astype(I16)
    l_sel = jnp.broadcast_to(jnp.where(take_all, I16_MIN + 1, l_thr), (PACK, tq)).astype(I16)
    k_off_v = lax.broadcasted_iota(I32, (tkv, tq), 0)
    q_pos_v = i * tq + lax.broadcasted_iota(I32, (tkv, tq), 1)

    def make_bias(c):
        sub = pl.ds(pl.multiple_of((c % (tk // tkv)) * tkv, tkv), tkv)
        hi = hi_ref[c // (tk // tkv), sub, :]
        lo = lo_ref[c // (tk // tkv), sub, :]

        @pl.when(exact)
        def _():
            for r in range(tkv // PACK):
                rows = slice(r * PACK, (r + 1) * PACK)
                sel = (hi[rows, :] > h_sel) | (lo[rows, :] >= l_sel)
                bias_ref[rows, :] = jnp.where(sel, jnp.zeros((PACK, tq), BF16),
                                              jnp.full((PACK, tq), NEG_BIG, BF16)).astype(F32)

        @pl.when(jnp.logical_not(exact))
        def _():
            hi32 = hi.astype(I32)
            lo32 = lo.astype(I32)
            is_h = hi32 == h_thr
            eq = is_h & (lo32 == l_thr)
            gt = (hi32 > h_thr) | (is_h & (lo32 > l_thr))
            tri = (lax.broadcasted_iota(I32, (tkv, tkv), 1)
                   <= lax.broadcasted_iota(I32, (tkv, tkv), 0))
            prefix = jnp.dot(jnp.where(tri, 1.0, 0.0).astype(BF16),
                             jnp.where(eq, 1.0, 0.0).astype(BF16),
                             preferred_element_type=F32) + run_ref[...]
            take = (gt | (eq & (prefix <= need_ref[...]))) & (c * tkv + k_off_v <= q_pos_v)
            bias_ref[...] = jnp.where(take, 0.0, NEG_BIG)
            run_ref[...] += jnp.sum(jnp.where(eq, 1.0, 0.0), axis=0, keepdims=True)

    def attention(online):
        if online:
            m_ref[...] = jnp.full_like(m_ref, NEG_BIG)
        l_ref[...] = jnp.zeros_like(l_ref)
        acc_ref[...] = jnp.zeros_like(acc_ref)

        def att_chunk(c, carry):
            make_bias(c)
            start = pl.multiple_of(c * tkv, tkv)
            for hd in range(N_ATTN_HEADS):
                s = jnp.dot(k_ref[hd // 2, pl.ds(start, tkv), :], qz[hd],
                            preferred_element_type=F32) + bias_ref[...]
                if online:
                    m_old = m_ref[hd]
                    m_new = jnp.maximum(m_old, jnp.max(s, axis=0, keepdims=True))
                    alpha = jnp.exp2(m_old - m_new)
                    p = jnp.exp2(s - m_new)
                    l_ref[hd] = alpha * l_ref[hd] + jnp.sum(p, axis=0, keepdims=True)
                    acc_ref[hd] = alpha * acc_ref[hd]
                    m_ref[hd] = m_new
                else:
                    p = jnp.exp2(s)
                    l_ref[hd] += jnp.sum(p, axis=0, keepdims=True)
                p_ref[hd] = p.astype(BF16)
            for hd in range(N_ATTN_HEADS):
                acc_ref[hd] += jnp.dot(vt_ref[c, hd * HEAD_DIM:(hd + 1) * HEAD_DIM, :], p_ref[hd],
                                       preferred_element_type=F32)
            return carry

        lax.fori_loop(0, (i * tq) // tkv + 1, att_chunk, 0)

    bounded = flag_ref[0] != 0

    @pl.when(bounded)
    def _():
        attention(False)

    @pl.when(jnp.logical_not(bounded))
    def _():
        attention(True)

    for pair in range(N_ATTN_HEADS // 2):
        both = jnp.concatenate([acc_ref[2 * pair] / l_ref[2 * pair],
                                acc_ref[2 * pair + 1] / l_ref[2 * pair + 1]], axis=0)
        o_ref[:, pair * LANES:(pair + 1) * LANES] = both.T.astype(BF16)


def _dsa_call(flag, q, k, vt, iq, ik, iwt, *, tq, tk):
    b, _, s, _ = k.shape
    tkv = vt.shape[-1]
    n_sel = min(TOPK_MAX, s // 4)
    tile_t = lambda w: pl.BlockSpec((None, w, tq), lambda bi, i, f: (bi, 0, i))
    whole = lambda *shape: pl.BlockSpec((None,) + shape, lambda bi, i, f: (bi,) + (0,) * len(shape),
                                        pipeline_mode=pl.Buffered(1))
    grid_spec = pltpu.PrefetchScalarGridSpec(
        num_scalar_prefetch=1,
        grid=(b, s // tq),
        in_specs=[tile_t(ATTN_W), tile_t(IDX_W), tile_t(SUBLANES),
                  whole(N_ATTN_HEADS // 2, s, LANES), whole(s // tkv, ATTN_W, tkv),
                  whole(s, LANES)],
        out_specs=pl.BlockSpec((None, tq, ATTN_W), lambda bi, i, f: (bi, i, 0)),
        scratch_shapes=[
            pltpu.VMEM((s // tk, tk, tq), I16),
            pltpu.VMEM((s // tk, tk, tq), I16),
            pltpu.VMEM((tkv, tq), F32),
            pltpu.VMEM((N_ATTN_HEADS, tkv, tq), BF16),
            pltpu.VMEM((N_ATTN_HEADS, 1, tq), F32),
            pltpu.VMEM((N_ATTN_HEADS, 1, tq), F32),
            pltpu.VMEM((N_ATTN_HEADS, HEAD_DIM, tq), F32),
            pltpu.VMEM((1, tq), F32),
            pltpu.VMEM((1, tq), F32),
        ])
    return pl.pallas_call(
        functools.partial(_dsa_kernel, tq=tq, tk=tk, tkv=tkv, n_sel=float(n_sel)),
        grid_spec=grid_spec,
        out_shape=jax.ShapeDtypeStruct((b, s, ATTN_W), BF16),
        compiler_params=pltpu.CompilerParams(
            dimension_semantics=("arbitrary", "arbitrary"), vmem_limit_bytes=VMEM_LIMIT_BYTES),
        name="dsa",
    )(flag, q, iq, iwt, k, vt, ik)


def _prep_w_in(w_in):
    offs = [sum(IN_SIZES[:n + 1]) for n in range(len(IN_SIZES) - 1)]
    aq, ak, av, iq, ik, iw, pu, mq = jnp.split(w_in, offs, axis=-1)
    pad = jnp.zeros(w_in.shape[:-1] + (LANES - N_IDX_HEADS,), w_in.dtype)
    return jnp.concatenate([aq, ak, av, iq, ik, ik, pu, mq, iw, pad], axis=-1).astype(BF16)


def _prep_ffn(gate, up, down, fc):
    depth, d, f = gate.shape
    split_cols = lambda w: w.astype(BF16).reshape(depth, d, f // fc, fc).transpose(0, 2, 1, 3)
    return split_cols(gate), split_cols(up), down.astype(BF16).reshape(depth, f // fc, fc, d)


def _block_diag(w_pool):
    depth, g, c, _ = w_pool.shape
    eye = jnp.eye(g, dtype=w_pool.dtype)
    return jnp.einsum("lgcd,gh->lgchd", w_pool, eye).reshape(depth, g * c, g * c).astype(BF16)


def kernel(x, mem, ffn1_norm, ffn1_gate, ffn1_up, ffn1_down, mix_norm, mem_norm, w_in, q_norm,
           k_norm, w_pool, pool_scale, w_mem_kv, mq_norm, mk_norm, w_out, ffn2_norm, ffn2_gate,
           ffn2_up, ffn2_down):
    b, s, d = x.shape
    depth = w_in.shape[0]
    tm = min(512, s)
    tq = min(256, s)
    tk = min(512, s)
    tkv = min(256, s)
    assert s % tm == 0 and s % tk == 0 and tk % tq == 0 and tk % tkv == 0 and tm % tkv == 0

    vec = lambda a, reps=1: jnp.tile(a, (1, reps))[:, None, :]
    ffn1 = _prep_ffn(ffn1_gate, ffn1_up, ffn1_down, MXU_N)
    ffn2 = _prep_ffn(ffn2_gate, ffn2_up, ffn2_down, MXU_N)
    w_in_p = _prep_w_in(w_in)
    w_out_b = w_out.astype(BF16)
    w_pool_bd = _block_diag(w_pool)
    head = jnp.arange(ATTN_W) // HEAD_DIM
    bd = jnp.where(head[:, None] == head[None, :], 1.0 / HEAD_DIM, 0.0).astype(BF16)
    qg, kg = vec(q_norm, N_ATTN_HEADS), vec(k_norm, N_ATTN_HEADS)
    mqg, mkg = vec(mq_norm, N_MEM_HEADS), vec(mk_norm, N_MEM_HEADS)
    f1g, f2g, mixg, memg, ps = vec(ffn1_norm), vec(ffn2_norm), vec(mix_norm), vec(mem_norm), vec(pool_scale)
    log2_bound = (NORM_SLACK * LOG2E * jnp.max(jnp.abs(q_norm), axis=-1)
                  * jnp.max(jnp.abs(k_norm), axis=-1))
    bounded = (log2_bound <= SAFE_LOG2_BOUND).astype(I32)

    mk, mv = _memkv_call(mem, memg, w_mem_kv.astype(BF16), mkg, bd)

    x2d = x.reshape(b * s, d)
    for l in range(depth):
        x2d = _ffn_call(x2d, l, f1g, *ffn1, tm=tm)
        q, k, vt, iq, ik, iwt, y_pm = _proj_call(
            x2d.reshape(b, s, d), l, mixg, w_in_p, qg, kg, mqg, bd, w_pool_bd, ps, mk, mv,
            tm=tm, tkv=tkv)
        y_attn = _dsa_call(bounded[l:l + 1], q, k, vt, iq, ik, iwt, tq=tq, tk=tk)
        x2d = _ffn_call(x2d, l, f2g, *ffn2, tm=tm,
                        mix=(y_attn.reshape(b * s, ATTN_W), y_pm.reshape(b * s, POOL_W + MEM_W),
                             w_out_b))
    return x2d.reshape(b, s, d)
```

```python
import functools
import math

import jax
import jax.numpy as jnp
from jax import lax
from jax.experimental import pallas as pl
from jax.experimental.pallas import tpu as pltpu

F32 = jnp.float32
BF16 = jnp.bfloat16
I32 = jnp.int32
I16 = jnp.int16

EPS = 1e-6
HEAD_DIM = 64
N_ATTN_HEADS = 8
N_IDX_HEADS = 4
N_POOL_GROUPS = 4
N_MEM_HEADS = 4
POOL_WINDOWS = (2, 4, 8, 16)
TOPK_MAX = 256

ATTN_W = N_ATTN_HEADS * HEAD_DIM
IDX_W = N_IDX_HEADS * HEAD_DIM
POOL_W = N_POOL_GROUPS * HEAD_DIM
MEM_W = N_MEM_HEADS * HEAD_DIM
IN_SIZES = (ATTN_W, ATTN_W, ATTN_W, IDX_W, HEAD_DIM, N_IDX_HEADS, POOL_W, MEM_W)

LANES = 128
SUBLANES = 8
PACK = 16
CNT_ROWS = 4 * PACK
MXU_N = 256
VMEM_LIMIT_BYTES = 60 * 1024 * 1024
INT_MIN = -(2 ** 31)
I16_MIN = -(2 ** 15)
I16_MAX = 2 ** 15 - 1
NEG_BIG = -1e30
LOG2E = math.log2(math.e)
NORM_SLACK = 8.1
SAFE_LOG2_BOUND = 80.0

OFF_Q, OFF_K, OFF_V = 0, ATTN_W, 2 * ATTN_W
OFF_IQ = 3 * ATTN_W
OFF_IK = OFF_IQ + IDX_W
OFF_PU = OFF_IK + LANES
OFF_MQ = OFF_PU + POOL_W
OFF_IW = OFF_MQ + MEM_W
D_IN_PAD = OFF_IW + LANES

NT_DIMS = (((1,), (1,)), ((), ()))


def _rms(x, g):
    return x * lax.rsqrt(jnp.mean(x * x, axis=-1, keepdims=True) + EPS) * g


def _head_rms(v, gain, bd):
    sq = v * v
    hi = sq.astype(BF16)
    lo = (sq - hi.astype(F32)).astype(BF16)
    ms = (jnp.dot(hi, bd, preferred_element_type=F32)
          + jnp.dot(lo, bd, preferred_element_type=F32))
    return v * lax.rsqrt(ms + EPS) * gain


def _half_masks(rows):
    lane = lax.broadcasted_iota(I32, (rows, LANES), 1)
    lo = lane < HEAD_DIM
    lo_b = jnp.where(lo, 1.0, 0.0).astype(BF16)
    hi_b = jnp.where(lo, 0.0, 1.0).astype(BF16)
    return lo, (lo_b, hi_b)


def _ffn_kernel(*refs, mixed, n_chunks):
    if mixed:
        (x_ref, ya_ref, yb_ref, wo_ref, g_ref, wg_ref, wu_ref, wd_ref,
         o_ref, h_ref, acc_ref) = refs
        x = (x_ref[...]
             + jnp.dot(ya_ref[...], wo_ref[:ATTN_W, :], preferred_element_type=F32)
             + jnp.dot(yb_ref[...], wo_ref[ATTN_W:, :], preferred_element_type=F32))
    else:
        x_ref, g_ref, wg_ref, wu_ref, wd_ref, o_ref, h_ref, acc_ref = refs
        x = x_ref[...]
    o_ref[...] = x
    h_ref[...] = _rms(x, g_ref[...]).astype(BF16)
    acc_ref[...] = jnp.zeros_like(acc_ref)

    def chunk(c, carry):
        h = h_ref[...]
        g = jnp.dot(h, wg_ref[c], preferred_element_type=F32)
        u = jnp.dot(h, wu_ref[c], preferred_element_type=F32)
        a = (g * (1.0 / (1.0 + jnp.exp(-g))) * u).astype(BF16)
        acc_ref[...] += jnp.dot(a, wd_ref[c], preferred_element_type=F32)
        return carry

    lax.fori_loop(0, n_chunks, chunk, 0, unroll=True)
    o_ref[...] += 0.5 * acc_ref[...]


def _ffn_call(x2d, layer, gain, wg, wu, wd, mix=None, *, tm):
    n, d = x2d.shape
    n_chunks, _, fc = wg.shape[1:]
    const = lambda *shape: pl.BlockSpec((None,) + shape, lambda i: (layer,) + (0,) * len(shape),
                                        pipeline_mode=pl.Buffered(1))
    row = lambda w: pl.BlockSpec((tm, w), lambda i: (i, 0))
    in_specs, args = [row(d)], [x2d]
    if mix is not None:
        ya, yb, wo = mix
        in_specs += [row(ya.shape[1]), row(yb.shape[1]), const(*wo.shape[1:])]
        args += [ya, yb, wo]
    in_specs += [const(1, d), const(n_chunks, d, fc), const(n_chunks, d, fc), const(n_chunks, fc, d)]
    args += [gain, wg, wu, wd]
    return pl.pallas_call(
        functools.partial(_ffn_kernel, mixed=mix is not None, n_chunks=n_chunks),
        grid=(n // tm,),
        in_specs=in_specs,
        out_specs=row(d),
        out_shape=jax.ShapeDtypeStruct((n, d), F32),
        scratch_shapes=[pltpu.VMEM((tm, d), BF16), pltpu.VMEM((tm, d), F32)],
        compiler_params=pltpu.CompilerParams(
            dimension_semantics=("arbitrary",), vmem_limit_bytes=VMEM_LIMIT_BYTES),
        name="ffn_mixed" if mix is not None else "ffn",
    )(*args)


def _memkv_kernel(mem_ref, g_ref, w_ref, kg_ref, bd_ref, mk_ref, mv_ref):
    mem_n = _rms(mem_ref[...], g_ref[...]).astype(BF16)
    kv = jnp.dot(mem_n, w_ref[...], preferred_element_type=F32)
    mk_ref[...] = _head_rms(kv[:, :MEM_W], kg_ref[...], bd_ref[:MEM_W, :MEM_W]).astype(BF16)
    mv_ref[...] = kv[:, MEM_W:].astype(BF16)


def _memkv_call(mem, mem_norm, w_kv, mk_gain, bd):
    b, m, d = mem.shape
    depth = w_kv.shape[0]
    out = jax.ShapeDtypeStruct((depth, b, m, MEM_W), BF16)
    out_spec = pl.BlockSpec((None, None, m, MEM_W), lambda l, i: (l, i, 0, 0))
    return pl.pallas_call(
        _memkv_kernel,
        grid=(depth, b),
        in_specs=[
            pl.BlockSpec((None, m, d), lambda l, i: (i, 0, 0)),
            pl.BlockSpec((None, 1, d), lambda l, i: (l, 0, 0)),
            pl.BlockSpec((None, d, 2 * MEM_W), lambda l, i: (l, 0, 0)),
            pl.BlockSpec((None, 1, MEM_W), lambda l, i: (l, 0, 0)),
            pl.BlockSpec(bd.shape, lambda l, i: (0, 0)),
        ],
        out_specs=[out_spec, out_spec],
        out_shape=[out, out],
        compiler_params=pltpu.CompilerParams(dimension_semantics=("arbitrary", "arbitrary")),
        name="memkv",
    )(mem, mem_norm, w_kv, mk_gain, bd)


POOL_HALO = 32


def _proj_kernel(x_ref, g_ref, w_ref, qg_ref, kg_ref, mqg_ref, bd_ref, wp_ref, ps_ref,
                 mk_ref, mv_ref,
                 q_ref, k_ref, vt_ref, iq_ref, ik_ref, iwt_ref, y_ref,
                 p_ref, s2_ref, s4_ref, s8_ref, *, tm, tkv):
    j = pl.program_id(1)
    rows = tm + POOL_HALO
    h = _rms(x_ref[...], g_ref[...]).astype(BF16)
    z = jnp.dot(h, w_ref[...], preferred_element_type=F32)
    bd = bd_ref[...]

    q_ref[...] = (_head_rms(z[:, OFF_Q:OFF_Q + ATTN_W], qg_ref[...], bd)
                  * (HEAD_DIM ** -0.5 * LOG2E)).T.astype(BF16)
    kn = _head_rms(z[:, OFF_K:OFF_K + ATTN_W], kg_ref[...], bd).astype(BF16)
    for pair in range(N_ATTN_HEADS // 2):
        k_ref[pair] = kn[:, pair * LANES:(pair + 1) * LANES]
    vt = z[:, OFF_V:OFF_V + ATTN_W].T
    for part in range(tm // tkv):
        vt_ref[part] = vt[:, part * tkv:(part + 1) * tkv].astype(BF16)
    iq_ref[...] = (z[:, OFF_IQ:OFF_IQ + IDX_W] * HEAD_DIM ** -0.5).T.astype(BF16)
    ik_ref[...] = z[:, OFF_IK:OFF_IK + LANES].astype(BF16)
    iwt_ref[...] = (z[:, OFF_IW:OFF_IW + LANES] * N_IDX_HEADS ** -0.5).T[0:SUBLANES, :]

    @pl.when(j == 0)
    def _():
        p_ref[0:POOL_HALO, :] = jnp.zeros((POOL_HALO, POOL_W), F32)

    pu = z[:, OFF_PU:OFF_PU + POOL_W]
    p_ref[POOL_HALO:rows, :] = pu
    s2_ref[8:rows, :] = p_ref[8:rows, :] + p_ref[7:rows - 1, :]
    s4_ref[16:rows, :] = s2_ref[16:rows, :] + s2_ref[14:rows - 2, :]
    s8_ref[24:rows, :] = s4_ref[24:rows, :] + s4_ref[20:rows - 4, :]
    s16 = s8_ref[POOL_HALO:rows, :] + s8_ref[24:rows - 8, :]
    lane = lax.broadcasted_iota(I32, (tm, POOL_W), 1)
    g1, g2, g3 = lane < HEAD_DIM, lane < 2 * HEAD_DIM, lane < 3 * HEAD_DIM
    wsum = jnp.where(g1, s2_ref[POOL_HALO:rows, :],
                     jnp.where(g2, s4_ref[POOL_HALO:rows, :],
                               jnp.where(g3, s8_ref[POOL_HALO:rows, :], s16)))
    win = jnp.where(g1, float(POOL_WINDOWS[0]),
                    jnp.where(g2, float(POOL_WINDOWS[1]),
                              jnp.where(g3, float(POOL_WINDOWS[2]), float(POOL_WINDOWS[3]))))
    t1 = (j * tm + 1 + lax.broadcasted_iota(I32, (tm, POOL_W), 0)).astype(F32)
    pooled = (wsum / jnp.minimum(t1, win) - pu).astype(BF16)
    y_pool = jnp.dot(pooled, wp_ref[...], preferred_element_type=F32) * ps_ref[...]
    y_ref[:, 0:POOL_W] = y_pool.astype(BF16)
    p_ref[16:POOL_HALO, :] = p_ref[tm + 16:rows, :]

    lo, half_b = _half_masks(tm)
    mqn = (_head_rms(z[:, OFF_MQ:OFF_MQ + MEM_W], mqg_ref[...], bd[:MEM_W, :MEM_W])
           * HEAD_DIM ** -0.5).astype(BF16)
    mk = mk_ref[...]
    mv = mv_ref[...]
    for pair in range(N_MEM_HEADS // 2):
        cols = slice(pair * LANES, (pair + 1) * LANES)
        outs = []
        for sub in range(2):
            logits = lax.dot_general(mqn[:, cols] * half_b[sub], mk[:, cols], NT_DIMS,
                                     preferred_element_type=F32)
            e = jnp.exp(logits - jnp.max(logits, axis=-1, keepdims=True))
            p = (e / jnp.sum(e, axis=-1, keepdims=True)).astype(BF16)
            outs.append(jnp.dot(p, mv[:, cols], preferred_element_type=F32))
        y_ref[:, POOL_W + pair * LANES:POOL_W + (pair + 1) * LANES] = (
            jnp.where(lo, outs[0], outs[1]).astype(BF16))


def _proj_call(x, layer, gain, w_in, qg, kg, mqg, bd, wp, ps, mk, mv, *, tm, tkv):
    b, s, d = x.shape
    m = mk.shape[2]
    lay = lambda *shape: pl.BlockSpec((None,) + shape, lambda i, j: (layer,) + (0,) * len(shape),
                                      pipeline_mode=pl.Buffered(1))
    tile = lambda w: pl.BlockSpec((None, tm, w), lambda i, j: (i, j, 0))
    mem = pl.BlockSpec((None, None, m, MEM_W), lambda i, j: (layer, i, 0, 0))
    sds = jax.ShapeDtypeStruct
    tile_t = lambda w: pl.BlockSpec((None, w, tm), lambda i, j: (i, 0, j))
    out_shape = [sds((b, ATTN_W, s), BF16),
                 sds((b, N_ATTN_HEADS // 2, s, LANES), BF16),
                 sds((b, s // tkv, ATTN_W, tkv), BF16),
                 sds((b, IDX_W, s), BF16),
                 sds((b, s, LANES), BF16),
                 sds((b, SUBLANES, s), F32),
                 sds((b, s, POOL_W + MEM_W), BF16)]
    out_specs = [tile_t(ATTN_W),
                 pl.BlockSpec((None, N_ATTN_HEADS // 2, tm, LANES), lambda i, j: (i, 0, j, 0)),
                 pl.BlockSpec((None, tm // tkv, ATTN_W, tkv), lambda i, j: (i, j, 0, 0)),
                 tile_t(IDX_W), tile(LANES), tile_t(SUBLANES),
                 tile(POOL_W + MEM_W)]
    rows = tm + POOL_HALO
    return pl.pallas_call(
        functools.partial(_proj_kernel, tm=tm, tkv=tkv),
        grid=(b, s // tm),
        in_specs=[tile(d), lay(1, d), lay(d, D_IN_PAD), lay(1, ATTN_W), lay(1, ATTN_W),
                  lay(1, MEM_W), pl.BlockSpec(bd.shape, lambda i, j: (0, 0)),
                  lay(POOL_W, POOL_W), lay(1, POOL_W), mem, mem],
        out_specs=out_specs,
        out_shape=out_shape,
        scratch_shapes=[pltpu.VMEM((rows, POOL_W), F32) for _ in range(4)],
        compiler_params=pltpu.CompilerParams(
            dimension_semantics=("arbitrary", "arbitrary"), vmem_limit_bytes=VMEM_LIMIT_BYTES),
        name="proj",
    )(x, gain, w_in, qg, kg, mqg, bd, wp, ps, mk, mv)


def _dsa_kernel(flag_ref, q_ref, iq_ref, iw_ref, k_ref, vt_ref, ik_ref, o_ref,
                hi_ref, lo_ref, bias_ref, p_ref, m_ref, l_ref, acc_ref, run_ref, need_ref,
                *, seq, tq, tk, tkv, n_sel):
    i = pl.program_id(1)
    nck = (i * tq) // tk + 1
    top = lax.broadcasted_iota(I32, (LANES, tq), 0) < HEAD_DIM
    half_b = (jnp.where(top, 1.0, 0.0).astype(BF16), jnp.where(top, 0.0, 1.0).astype(BF16))
    q_pos = i * tq + lax.broadcasted_iota(I32, (tk, tq), 1)
    k_off = lax.broadcasted_iota(I32, (tk, tq), 0)

    iq = iq_ref[...]
    iw = iw_ref[...]
    iqz = [iq[(hd // 2) * LANES:(hd // 2 + 1) * LANES, :] * half_b[hd % 2]
           for hd in range(N_IDX_HEADS)]
    wrow = [iw[hd:hd + 1, :] for hd in range(N_IDX_HEADS)]

    def score_chunk(c, causal):
        ikc = ik_ref[pl.ds(pl.multiple_of(c * tk, tk), tk), :]
        score = None
        for hd in range(N_IDX_HEADS):
            d = jnp.dot(ikc, iqz[hd], preferred_element_type=F32)
            t = jnp.maximum(d, 0.0) * wrow[hd]
            score = t if score is None else score + t
        bits = pltpu.bitcast(score, I32)
        neg = bits >> 31
        key = (bits ^ (neg & 0x7FFFFFFF)) - neg
        key = jnp.where(key == 0, (seq - 1 - c * tk) - k_off, key + (~neg & seq))
        if causal:
            key = jnp.where(c * tk + k_off <= q_pos, key, INT_MIN)
        hi_ref[c] = (key >> 16).astype(I16)
        lo_ref[c] = (key ^ 0x8000).astype(I16)
        return 0

    lax.fori_loop(0, nck - 1, lambda c, _: score_chunk(c, False), 0)
    score_chunk(nck - 1, True)

    def count_ge(ref, cand):
        cand_b = jnp.broadcast_to(cand, (CNT_ROWS, tq)).astype(I16)
        one = jnp.ones((CNT_ROWS, tq), I16)
        zero = jnp.zeros((CNT_ROWS, tq), I16)

        def cnt_chunk(c, acc):
            a = ref[c]
            for r in range(tk // CNT_ROWS):
                acc = acc + jnp.where(a[r * CNT_ROWS:(r + 1) * CNT_ROWS, :] >= cand_b, one, zero)
            return acc

        acc = lax.fori_loop(0, nck, cnt_chunk, zero)
        return jnp.sum(acc.astype(I32).astype(F32), axis=0, keepdims=True)

    def greedy(ref, target, cnt0):
        def step(it, carry):
            t_u, cnt_t = carry
            cand_u = t_u | lax.shift_left(jnp.int32(1), 15 - it)
            cnt = count_ge(ref, cand_u + I16_MIN)
            ok = cnt >= target
            return jnp.where(ok, cand_u, t_u), jnp.where(ok, cnt, cnt_t)
        t_u, cnt_t = lax.fori_loop(0, 16, step, (jnp.zeros((1, tq), I32), cnt0))
        return t_u + I16_MIN, cnt_t

    total = jnp.full((1, tq), 1.0, F32) * (nck * tk).astype(F32)
    h_thr, cnt_ge_h = greedy(hi_ref, n_sel, total)
    cnt_gt_h = jnp.where(h_thr == I16_MAX, 0.0,
                         count_ge(hi_ref, jnp.minimum(h_thr + 1, I16_MAX)))
    h_b = jnp.broadcast_to(h_thr, (PACK, tq)).astype(I16)

    def mask_chunk(c, carry):
        hi = hi_ref[c]
        lo = lo_ref[c]
        for r in range(tk // PACK):
            rows = slice(r * PACK, (r + 1) * PACK)
            lo_ref[c, rows, :] = jnp.where(hi[rows, :] == h_b, lo[rows, :],
                                           jnp.full((PACK, tq), I16_MIN, I16))
        return carry

    lax.fori_loop(0, nck, mask_chunk, 0)
    l_thr, cnt_ge_l = greedy(lo_ref, n_sel - cnt_gt_h, cnt_ge_h - cnt_gt_h)
    n_ge = cnt_gt_h + cnt_ge_l
    bad = jnp.maximum(jnp.abs(n_ge - n_sel), jnp.where(h_thr == I16_MIN, 1.0, 0.0))
    exact = jnp.max(bad) == 0.0

    @pl.when(jnp.logical_not(exact))
    def _():
        cnt_gt_l = jnp.where(l_thr == I16_MAX, 0.0,
                             count_ge(lo_ref, jnp.minimum(l_thr + 1, I16_MAX)))
        need_ref[...] = n_sel - cnt_gt_h - cnt_gt_l
        run_ref[...] = jnp.zeros_like(run_ref)

    q = q_ref[...]
    qz = [q[(hd // 2) * LANES:(hd // 2 + 1) * LANES, :] * half_b[hd % 2]
          for hd in range(N_ATTN_HEADS)]
    take_all = l_thr == I16_MIN
    h_sel = jnp.broadcast_to(jnp.where(take_all, h_thr - 1, h_thr), (PACK, tq)).astype(I16)
    l_sel = jnp.broadcast_to(jnp.where(take_all, I16_MIN + 1, l_thr), (PACK, tq)).astype(I16)
    k_off_v = lax.broadcasted_iota(I32, (tkv, tq), 0)
    q_pos_v = i * tq + lax.broadcasted_iota(I32, (tkv, tq), 1)

    def make_bias(c):
        sub = pl.ds(pl.multiple_of((c % (tk // tkv)) * tkv, tkv), tkv)
        hi = hi_ref[c // (tk // tkv), sub, :]
        lo = lo_ref[c // (tk // tkv), sub, :]

        @pl.when(exact)
        def _():
            for r in range(tkv // PACK):
                rows = slice(r * PACK, (r + 1) * PACK)
                sel = (hi[rows, :] > h_sel) | (lo[rows, :] >= l_sel)
                bias_ref[rows, :] = jnp.where(sel, jnp.zeros((PACK, tq), BF16),
                                              jnp.full((PACK, tq), NEG_BIG, BF16)).astype(F32)

        @pl.when(jnp.logical_not(exact))
        def _():
            hi32 = hi.astype(I32)
            lo32 = lo.astype(I32)
            is_h = hi32 == h_thr
            eq = is_h & (lo32 == l_thr)
            gt = (hi32 > h_thr) | (is_h & (lo32 > l_thr))
            tri = (lax.broadcasted_iota(I32, (tkv, tkv), 1)
                   <= lax.broadcasted_iota(I32, (tkv, tkv), 0))
            prefix = jnp.dot(jnp.where(tri, 1.0, 0.0).astype(BF16),
                             jnp.where(eq, 1.0, 0.0).astype(BF16),
                             preferred_element_type=F32) + run_ref[...]
            take = (gt | (eq & (prefix <= need_ref[...]))) & (c * tkv + k_off_v <= q_pos_v)
            bias_ref[...] = jnp.where(take, 0.0, NEG_BIG)
            run_ref[...] += jnp.sum(jnp.where(eq, 1.0, 0.0), axis=0, keepdims=True)

    def attention(online):
        if online:
            m_ref[...] = jnp.full_like(m_ref, NEG_BIG)
        l_ref[...] = jnp.zeros_like(l_ref)
        acc_ref[...] = jnp.zeros_like(acc_ref)

        def att_chunk(c, carry):
            make_bias(c)
            start = pl.multiple_of(c * tkv, tkv)
            for hd in range(N_ATTN_HEADS):
                s = jnp.dot(k_ref[hd // 2, pl.ds(start, tkv), :], qz[hd],
                            preferred_element_type=F32) + bias_ref[...]
                if online:
                    m_old = m_ref[hd]
                    m_new = jnp.maximum(m_old, jnp.max(s, axis=0, keepdims=True))
                    alpha = jnp.exp2(m_old - m_new)
                    p = jnp.exp2(s - m_new)
                    l_ref[hd] = alpha * l_ref[hd] + jnp.sum(p, axis=0, keepdims=True)
                    acc_ref[hd] = alpha * acc_ref[hd]
                    m_ref[hd] = m_new
                else:
                    p = jnp.exp2(s)
                    l_ref[hd] += jnp.sum(p, axis=0, keepdims=True)
                p_ref[hd] = p.astype(BF16)
            for hd in range(N_ATTN_HEADS):
                acc_ref[hd] += jnp.dot(vt_ref[c, hd * HEAD_DIM:(hd + 1) * HEAD_DIM, :], p_ref[hd],
                                       preferred_element_type=F32)
            return carry

        lax.fori_loop(0, (i * tq) // tkv + 1, att_chunk, 0)

    bounded = flag_ref[0] != 0

    @pl.when(bounded)
    def _():
        attention(False)

    @pl.when(jnp.logical_not(bounded))
    def _():
        attention(True)

    for pair in range(N_ATTN_HEADS // 2):
        both = jnp.concatenate([acc_ref[2 * pair] / l_ref[2 * pair],
                                acc_ref[2 * pair + 1] / l_ref[2 * pair + 1]], axis=0)
        o_ref[:, pair * LANES:(pair + 1) * LANES] = both.T.astype(BF16)


def _dsa_call(flag, q, k, vt, iq, ik, iwt, *, tq, tk):
    b, _, s, _ = k.shape
    tkv = vt.shape[-1]
    n_sel = min(TOPK_MAX, s // 4)
    tile_t = lambda w: pl.BlockSpec((None, w, tq), lambda bi, i, f: (bi, 0, i))
    whole = lambda *shape: pl.BlockSpec((None,) + shape, lambda bi, i, f: (bi,) + (0,) * len(shape),
                                        pipeline_mode=pl.Buffered(1))
    grid_spec = pltpu.PrefetchScalarGridSpec(
        num_scalar_prefetch=1,
        grid=(b, s // tq),
        in_specs=[tile_t(ATTN_W), tile_t(IDX_W), tile_t(SUBLANES),
                  whole(N_ATTN_HEADS // 2, s, LANES), whole(s // tkv, ATTN_W, tkv),
                  whole(s, LANES)],
        out_specs=pl.BlockSpec((None, tq, ATTN_W), lambda bi, i, f: (bi, i, 0)),
        scratch_shapes=[
            pltpu.VMEM((s // tk, tk, tq), I16),
            pltpu.VMEM((s // tk, tk, tq), I16),
            pltpu.VMEM((tkv, tq), F32),
            pltpu.VMEM((N_ATTN_HEADS, tkv, tq), BF16),
            pltpu.VMEM((N_ATTN_HEADS, 1, tq), F32),
            pltpu.VMEM((N_ATTN_HEADS, 1, tq), F32),
            pltpu.VMEM((N_ATTN_HEADS, HEAD_DIM, tq), F32),
            pltpu.VMEM((1, tq), F32),
            pltpu.VMEM((1, tq), F32),
        ])
    return pl.pallas_call(
        functools.partial(_dsa_kernel, seq=s, tq=tq, tk=tk, tkv=tkv, n_sel=float(n_sel)),
        grid_spec=grid_spec,
        out_shape=jax.ShapeDtypeStruct((b, s, ATTN_W), BF16),
        compiler_params=pltpu.CompilerParams(
            dimension_semantics=("arbitrary", "arbitrary"), vmem_limit_bytes=VMEM_LIMIT_BYTES),
        name="dsa",
    )(flag, q, iq, iwt, k, vt, ik)


def _prep_w_in(w_in):
    offs = [sum(IN_SIZES[:n + 1]) for n in range(len(IN_SIZES) - 1)]
    aq, ak, av, iq, ik, iw, pu, mq = jnp.split(w_in, offs, axis=-1)
    pad = jnp.zeros(w_in.shape[:-1] + (LANES - N_IDX_HEADS,), w_in.dtype)
    return jnp.concatenate([aq, ak, av, iq, ik, ik, pu, mq, iw, pad], axis=-1).astype(BF16)


def _prep_ffn(gate, up, down, fc):
    depth, d, f = gate.shape
    split_cols = lambda w: w.astype(BF16).reshape(depth, d, f // fc, fc).transpose(0, 2, 1, 3)
    return split_cols(gate), split_cols(up), down.astype(BF16).reshape(depth, f // fc, fc, d)


def _block_diag(w_pool):
    depth, g, c, _ = w_pool.shape
    eye = jnp.eye(g, dtype=w_pool.dtype)
    return jnp.einsum("lgcd,gh->lgchd", w_pool, eye).reshape(depth, g * c, g * c).astype(BF16)


def kernel(x, mem, ffn1_norm, ffn1_gate, ffn1_up, ffn1_down, mix_norm, mem_norm, w_in, q_norm,
           k_norm, w_pool, pool_scale, w_mem_kv, mq_norm, mk_norm, w_out, ffn2_norm, ffn2_gate,
           ffn2_up, ffn2_down):
    b, s, d = x.shape
    depth = w_in.shape[0]
    tm = min(512, s)
    tq = min(256, s)
    tk = min(512, s)
    tkv = min(256, s)
    assert s % tm == 0 and s % tk == 0 and tk % tq == 0 and tk % tkv == 0 and tm % tkv == 0

    vec = lambda a, reps=1: jnp.tile(a, (1, reps))[:, None, :]
    ffn1 = _prep_ffn(ffn1_gate, ffn1_up, ffn1_down, MXU_N)
    ffn2 = _prep_ffn(ffn2_gate, ffn2_up, ffn2_down, MXU_N)
    w_in_p = _prep_w_in(w_in)
    w_out_b = w_out.astype(BF16)
    w_pool_bd = _block_diag(w_pool)
    head = jnp.arange(ATTN_W) // HEAD_DIM
    bd = jnp.where(head[:, None] == head[None, :], 1.0 / HEAD_DIM, 0.0).astype(BF16)
    qg, kg = vec(q_norm, N_ATTN_HEADS), vec(k_norm, N_ATTN_HEADS)
    mqg, mkg = vec(mq_norm, N_MEM_HEADS), vec(mk_norm, N_MEM_HEADS)
    f1g, f2g, mixg, memg, ps = vec(ffn1_norm), vec(ffn2_norm), vec(mix_norm), vec(mem_norm), vec(pool_scale)
    log2_bound = (NORM_SLACK * LOG2E * jnp.max(jnp.abs(q_norm), axis=-1)
                  * jnp.max(jnp.abs(k_norm), axis=-1))
    bounded = (log2_bound <= SAFE_LOG2_BOUND).astype(I32)

    mk, mv = _memkv_call(mem, memg, w_mem_kv.astype(BF16), mkg, bd)

    x2d = x.reshape(b * s, d)
    for l in range(depth):
        x2d = _ffn_call(x2d, l, f1g, *ffn1, tm=tm)
        q, k, vt, iq, ik, iwt, y_pm = _proj_call(
            x2d.reshape(b, s, d), l, mixg, w_in_p, qg, kg, mqg, bd, w_pool_bd, ps, mk, mv,
            tm=tm, tkv=tkv)
        y_attn = _dsa_call(bounded[l:l + 1], q, k, vt, iq, ik, iwt, tq=tq, tk=tk)
        x2d = _ffn_call(x2d, l, f2g, *ffn2, tm=tm,
                        mix=(y_attn.reshape(b * s, ATTN_W), y_pm.reshape(b * s, POOL_W + MEM_W),
                             w_out_b))
    return x2d.reshape(b, s, d)
```

```python
import functools
import math

import jax
import jax.numpy as jnp
from jax import lax
from jax.experimental import pallas as pl
from jax.experimental.pallas import tpu as pltpu

F32 = jnp.float32
BF16 = jnp.bfloat16
I32 = jnp.int32
I16 = jnp.int16

EPS = 1e-6
HEAD_DIM = 64
N_ATTN_HEADS = 8
N_IDX_HEADS = 4
N_POOL_GROUPS = 4
N_MEM_HEADS = 4
POOL_WINDOWS = (2, 4, 8, 16)
TOPK_MAX = 256

ATTN_W = N_ATTN_HEADS * HEAD_DIM
IDX_W = N_IDX_HEADS * HEAD_DIM
POOL_W = N_POOL_GROUPS * HEAD_DIM
MEM_W = N_MEM_HEADS * HEAD_DIM
IN_SIZES = (ATTN_W, ATTN_W, ATTN_W, IDX_W, HEAD_DIM, N_IDX_HEADS, POOL_W, MEM_W)

LANES = 128
SUBLANES = 8
PACK = 16
CNT_ROWS = 4 * PACK
MXU_N = 256
VMEM_LIMIT_BYTES = 60 * 1024 * 1024
INT_MIN = -(2 ** 31)
I16_MIN = -(2 ** 15)
I16_MAX = 2 ** 15 - 1
NEG_BIG = -1e30
LOG2E = math.log2(math.e)
NORM_SLACK = 8.1
SAFE_LOG2_BOUND = 80.0

OFF_Q, OFF_K, OFF_V = 0, ATTN_W, 2 * ATTN_W
OFF_IQ = 3 * ATTN_W
OFF_IK = OFF_IQ + IDX_W
OFF_PU = OFF_IK + LANES
OFF_MQ = OFF_PU + POOL_W
OFF_IW = OFF_MQ + MEM_W
D_IN_PAD = OFF_IW + LANES

NT_DIMS = (((1,), (1,)), ((), ()))


def _rms(x, g):
    return x * lax.rsqrt(jnp.mean(x * x, axis=-1, keepdims=True) + EPS) * g


def _head_rms(v, gain, bd):
    sq = v * v
    hi = sq.astype(BF16)
    lo = (sq - hi.astype(F32)).astype(BF16)
    ms = (jnp.dot(hi, bd, preferred_element_type=F32)
          + jnp.dot(lo, bd, preferred_element_type=F32))
    return v * lax.rsqrt(ms + EPS) * gain


def _half_masks(rows):
    lane = lax.broadcasted_iota(I32, (rows, LANES), 1)
    lo = lane < HEAD_DIM
    lo_b = jnp.where(lo, 1.0, 0.0).astype(BF16)
    hi_b = jnp.where(lo, 0.0, 1.0).astype(BF16)
    return lo, (lo_b, hi_b)


def _ffn_kernel(*refs, mixed, n_chunks):
    if mixed:
        (x_ref, ya_ref, yb_ref, wo_ref, g_ref, wg_ref, wu_ref, wd_ref,
         o_ref, h_ref, acc_ref) = refs
        x = (x_ref[...]
             + jnp.dot(ya_ref[...], wo_ref[:ATTN_W, :], preferred_element_type=F32)
             + jnp.dot(yb_ref[...], wo_ref[ATTN_W:, :], preferred_element_type=F32))
    else:
        x_ref, g_ref, wg_ref, wu_ref, wd_ref, o_ref, h_ref, acc_ref = refs
        x = x_ref[...]
    o_ref[...] = x
    h_ref[...] = _rms(x, g_ref[...]).astype(BF16)
    acc_ref[...] = jnp.zeros_like(acc_ref)

    def chunk(c, carry):
        h = h_ref[...]
        g = jnp.dot(h, wg_ref[c], preferred_element_type=F32)
        u = jnp.dot(h, wu_ref[c], preferred_element_type=F32)
        a = (g * (1.0 / (1.0 + jnp.exp(-g))) * u).astype(BF16)
        acc_ref[...] += jnp.dot(a, wd_ref[c], preferred_element_type=F32)
        return carry

    lax.fori_loop(0, n_chunks, chunk, 0, unroll=True)
    o_ref[...] += 0.5 * acc_ref[...]


def _ffn_call(x2d, layer, gain, wg, wu, wd, mix=None, *, tm):
    n, d = x2d.shape
    n_chunks, _, fc = wg.shape[1:]
    const = lambda *shape: pl.BlockSpec((None,) + shape, lambda i: (layer,) + (0,) * len(shape),
                                        pipeline_mode=pl.Buffered(1))
    row = lambda w: pl.BlockSpec((tm, w), lambda i: (i, 0))
    in_specs, args = [row(d)], [x2d]
    if mix is not None:
        ya, yb, wo = mix
        in_specs += [row(ya.shape[1]), row(yb.shape[1]), const(*wo.shape[1:])]
        args += [ya, yb, wo]
    in_specs += [const(1, d), const(n_chunks, d, fc), const(n_chunks, d, fc), const(n_chunks, fc, d)]
    args += [gain, wg, wu, wd]
    return pl.pallas_call(
        functools.partial(_ffn_kernel, mixed=mix is not None, n_chunks=n_chunks),
        grid=(n // tm,),
        in_specs=in_specs,
        out_specs=row(d),
        out_shape=jax.ShapeDtypeStruct((n, d), F32),
        scratch_shapes=[pltpu.VMEM((tm, d), BF16), pltpu.VMEM((tm, d), F32)],
        compiler_params=pltpu.CompilerParams(
            dimension_semantics=("arbitrary",), vmem_limit_bytes=VMEM_LIMIT_BYTES),
        name="ffn_mixed" if mix is not None else "ffn",
    )(*args)


def _memkv_kernel(mem_ref, g_ref, w_ref, kg_ref, bd_ref, mk_ref, mv_ref):
    mem_n = _rms(mem_ref[...], g_ref[...]).astype(BF16)
    kv = jnp.dot(mem_n, w_ref[...], preferred_element_type=F32)
    mk_ref[...] = _head_rms(kv[:, :MEM_W], kg_ref[...], bd_ref[:MEM_W, :MEM_W]).astype(BF16)
    mv_ref[...] = kv[:, MEM_W:].astype(BF16)


def _memkv_call(mem, mem_norm, w_kv, mk_gain, bd):
    b, m, d = mem.shape
    depth = w_kv.shape[0]
    out = jax.ShapeDtypeStruct((depth, b, m, MEM_W), BF16)
    out_spec = pl.BlockSpec((None, None, m, MEM_W), lambda l, i: (l, i, 0, 0))
    return pl.pallas_call(
        _memkv_kernel,
        grid=(depth, b),
        in_specs=[
            pl.BlockSpec((None, m, d), lambda l, i: (i, 0, 0)),
            pl.BlockSpec((None, 1, d), lambda l, i: (l, 0, 0)),
            pl.BlockSpec((None, d, 2 * MEM_W), lambda l, i: (l, 0, 0)),
            pl.BlockSpec((None, 1, MEM_W), lambda l, i: (l, 0, 0)),
            pl.BlockSpec(bd.shape, lambda l, i: (0, 0)),
        ],
        out_specs=[out_spec, out_spec],
        out_shape=[out, out],
        compiler_params=pltpu.CompilerParams(dimension_semantics=("arbitrary", "arbitrary")),
        name="memkv",
    )(mem, mem_norm, w_kv, mk_gain, bd)


POOL_HALO = 32


def _proj_kernel(x_ref, g_ref, w_ref, qg_ref, kg_ref, mqg_ref, bd_ref, wp_ref, ps_ref,
                 mk_ref, mv_ref,
                 q_ref, k_ref, vt_ref, iq_ref, ik_ref, iwt_ref, y_ref,
                 p_ref, s2_ref, s4_ref, s8_ref, *, tm, tkv):
    j = pl.program_id(1)
    rows = tm + POOL_HALO
    h = _rms(x_ref[...], g_ref[...]).astype(BF16)
    z = jnp.dot(h, w_ref[...], preferred_element_type=F32)
    bd = bd_ref[...]

    q_ref[...] = (_head_rms(z[:, OFF_Q:OFF_Q + ATTN_W], qg_ref[...], bd)
                  * (HEAD_DIM ** -0.5 * LOG2E)).T.astype(BF16)
    kn = _head_rms(z[:, OFF_K:OFF_K + ATTN_W], kg_ref[...], bd).astype(BF16)
    for pair in range(N_ATTN_HEADS // 2):
        k_ref[pair] = kn[:, pair * LANES:(pair + 1) * LANES]
    vt = z[:, OFF_V:OFF_V + ATTN_W].T
    for part in range(tm // tkv):
        vt_ref[part] = vt[:, part * tkv:(part + 1) * tkv].astype(BF16)
    iq_ref[...] = (z[:, OFF_IQ:OFF_IQ + IDX_W] * HEAD_DIM ** -0.5).T.astype(BF16)
    ik_ref[...] = z[:, OFF_IK:OFF_IK + LANES].astype(BF16)
    iwt_ref[...] = (z[:, OFF_IW:OFF_IW + LANES] * N_IDX_HEADS ** -0.5).T[0:SUBLANES, :]

    @pl.when(j == 0)
    def _():
        p_ref[0:POOL_HALO, :] = jnp.zeros((POOL_HALO, POOL_W), F32)

    pu = z[:, OFF_PU:OFF_PU + POOL_W]
    p_ref[POOL_HALO:rows, :] = pu
    s2_ref[8:rows, :] = p_ref[8:rows, :] + p_ref[7:rows - 1, :]
    s4_ref[16:rows, :] = s2_ref[16:rows, :] + s2_ref[14:rows - 2, :]
    s8_ref[24:rows, :] = s4_ref[24:rows, :] + s4_ref[20:rows - 4, :]
    s16 = s8_ref[POOL_HALO:rows, :] + s8_ref[24:rows - 8, :]
    lane = lax.broadcasted_iota(I32, (tm, POOL_W), 1)
    g1, g2, g3 = lane < HEAD_DIM, lane < 2 * HEAD_DIM, lane < 3 * HEAD_DIM
    wsum = jnp.where(g1, s2_ref[POOL_HALO:rows, :],
                     jnp.where(g2, s4_ref[POOL_HALO:rows, :],
                               jnp.where(g3, s8_ref[POOL_HALO:rows, :], s16)))
    win = jnp.where(g1, float(POOL_WINDOWS[0]),
                    jnp.where(g2, float(POOL_WINDOWS[1]),
                              jnp.where(g3, float(POOL_WINDOWS[2]), float(POOL_WINDOWS[3]))))
    t1 = (j * tm + 1 + lax.broadcasted_iota(I32, (tm, POOL_W), 0)).astype(F32)
    pooled = (wsum / jnp.minimum(t1, win) - pu).astype(BF16)
    y_pool = jnp.dot(pooled, wp_ref[...], preferred_element_type=F32) * ps_ref[...]
    y_ref[:, 0:POOL_W] = y_pool.astype(BF16)
    p_ref[16:POOL_HALO, :] = p_ref[tm + 16:rows, :]

    lo, half_b = _half_masks(tm)
    mqn = (_head_rms(z[:, OFF_MQ:OFF_MQ + MEM_W], mqg_ref[...], bd[:MEM_W, :MEM_W])
           * HEAD_DIM ** -0.5).astype(BF16)
    mk = mk_ref[...]
    mv = mv_ref[...]
    for pair in range(N_MEM_HEADS // 2):
        cols = slice(pair * LANES, (pair + 1) * LANES)
        outs = []
        for sub in range(2):
            logits = lax.dot_general(mqn[:, cols] * half_b[sub], mk[:, cols], NT_DIMS,
                                     preferred_element_type=F32)
            e = jnp.exp(logits - jnp.max(logits, axis=-1, keepdims=True))
            p = (e / jnp.sum(e, axis=-1, keepdims=True)).astype(BF16)
            outs.append(jnp.dot(p, mv[:, cols], preferred_element_type=F32))
        y_ref[:, POOL_W + pair * LANES:POOL_W + (pair + 1) * LANES] = (
            jnp.where(lo, outs[0], outs[1]).astype(BF16))


def _proj_call(x, layer, gain, w_in, qg, kg, mqg, bd, wp, ps, mk, mv, *, tm, tkv):
    b, s, d = x.shape
    m = mk.shape[2]
    lay = lambda *shape: pl.BlockSpec((None,) + shape, lambda i, j: (layer,) + (0,) * len(shape),
                                      pipeline_mode=pl.Buffered(1))
    tile = lambda w: pl.BlockSpec((None, tm, w), lambda i, j: (i, j, 0))
    mem = pl.BlockSpec((None, None, m, MEM_W), lambda i, j: (layer, i, 0, 0))
    sds = jax.ShapeDtypeStruct
    tile_t = lambda w: pl.BlockSpec((None, w, tm), lambda i, j: (i, 0, j))
    out_shape = [sds((b, ATTN_W, s), BF16),
                 sds((b, N_ATTN_HEADS // 2, s, LANES), BF16),
                 sds((b, s // tkv, ATTN_W, tkv), BF16),
                 sds((b, IDX_W, s), BF16),
                 sds((b, s, LANES), BF16),
                 sds((b, SUBLANES, s), F32),
                 sds((b, s, POOL_W + MEM_W), BF16)]
    out_specs = [tile_t(ATTN_W),
                 pl.BlockSpec((None, N_ATTN_HEADS // 2, tm, LANES), lambda i, j: (i, 0, j, 0)),
                 pl.BlockSpec((None, tm // tkv, ATTN_W, tkv), lambda i, j: (i, j, 0, 0)),
                 tile_t(IDX_W), tile(LANES), tile_t(SUBLANES),
                 tile(POOL_W + MEM_W)]
    rows = tm + POOL_HALO
    return pl.pallas_call(
        functools.partial(_proj_kernel, tm=tm, tkv=tkv),
        grid=(b, s // tm),
        in_specs=[tile(d), lay(1, d), lay(d, D_IN_PAD), lay(1, ATTN_W), lay(1, ATTN_W),
                  lay(1, MEM_W), pl.BlockSpec(bd.shape, lambda i, j: (0, 0)),
                  lay(POOL_W, POOL_W), lay(1, POOL_W), mem, mem],
        out_specs=out_specs,
        out_shape=out_shape,
        scratch_shapes=[pltpu.VMEM((rows, POOL_W), F32) for _ in range(4)],
        compiler_params=pltpu.CompilerParams(
            dimension_semantics=("arbitrary", "arbitrary"), vmem_limit_bytes=VMEM_LIMIT_BYTES),
        name="proj",
    )(x, gain, w_in, qg, kg, mqg, bd, wp, ps, mk, mv)


def _dsa_kernel(flag_ref, q_ref, iq_ref, iw_ref, k_ref, vt_ref, ik_ref, o_ref,
                hi_ref, lo_ref, bias_ref, p_ref, m_ref, l_ref, acc_ref, run_ref, need_ref,
                *, seq, tq, tk, tkv, n_sel):
    i = pl.program_id(1)
    nck = (i * tq) // tk + 1
    top = lax.broadcasted_iota(I32, (LANES, tq), 0) < HEAD_DIM
    half_b = (jnp.where(top, 1.0, 0.0).astype(BF16), jnp.where(top, 0.0, 1.0).astype(BF16))
    q_pos = i * tq + lax.broadcasted_iota(I32, (tk, tq), 1)
    k_off = lax.broadcasted_iota(I32, (tk, tq), 0)

    iq = iq_ref[...]
    iw = iw_ref[...]
    iqz = [iq[(hd // 2) * LANES:(hd // 2 + 1) * LANES, :] * half_b[hd % 2]
           for hd in range(N_IDX_HEADS)]
    wrow = [iw[hd:hd + 1, :] for hd in range(N_IDX_HEADS)]

    def score_chunk(c, causal):
        ikc = ik_ref[pl.ds(pl.multiple_of(c * tk, tk), tk), :]
        score = None
        for hd in range(N_IDX_HEADS):
            d = jnp.dot(ikc, iqz[hd], preferred_element_type=F32)
            t = jnp.maximum(d, 0.0) * wrow[hd]
            score = t if score is None else score + t
        bits = pltpu.bitcast(score, I32)
        neg = bits >> 31
        key = (bits ^ (neg & 0x7FFFFFFF)) - neg
        key = jnp.where(key == 0, (seq - 1 - c * tk) - k_off, key + (~neg & seq))
        if causal:
            key = jnp.where(c * tk + k_off <= q_pos, key, INT_MIN)
        hi_ref[c] = (key >> 16).astype(I16)
        lo_ref[c] = (key ^ 0x8000).astype(I16)
        return 0

    lax.fori_loop(0, nck - 1, lambda c, _: score_chunk(c, False), 0)
    score_chunk(nck - 1, True)

    def count_ge(ref, cand):
        cand_b = jnp.broadcast_to(cand, (CNT_ROWS, tq)).astype(I16)
        one = jnp.ones((CNT_ROWS, tq), I16)
        zero = jnp.zeros((CNT_ROWS, tq), I16)

        def cnt_chunk(c, acc):
            a = ref[c]
            for r in range(tk // CNT_ROWS):
                acc = acc + jnp.where(a[r * CNT_ROWS:(r + 1) * CNT_ROWS, :] >= cand_b, one, zero)
            return acc

        acc = lax.fori_loop(0, nck, cnt_chunk, zero)
        return jnp.sum(acc.astype(I32).astype(F32), axis=0, keepdims=True)

    def greedy(ref, target, cnt0):
        def step(it, carry):
            t_u, cnt_t = carry
            cand_u = t_u | lax.shift_left(jnp.int32(1), 15 - it)
            cnt = count_ge(ref, cand_u + I16_MIN)
            ok = cnt >= target
            return jnp.where(ok, cand_u, t_u), jnp.where(ok, cnt, cnt_t)
        t_u, cnt_t = lax.fori_loop(0, 16, step, (jnp.zeros((1, tq), I32), cnt0))
        return t_u + I16_MIN, cnt_t

    total = jnp.full((1, tq), 1.0, F32) * (nck * tk).astype(F32)
    h_thr, cnt_ge_h = greedy(hi_ref, n_sel, total)
    cnt_gt_h = jnp.where(h_thr == I16_MAX, 0.0,
                         count_ge(hi_ref, jnp.minimum(h_thr + 1, I16_MAX)))
    h_b = jnp.broadcast_to(h_thr, (PACK, tq)).astype(I16)

    def mask_chunk(c, carry):
        hi = hi_ref[c]
        lo = lo_ref[c]
        for r in range(tk // PACK):
            rows = slice(r * PACK, (r + 1) * PACK)
            lo_ref[c, rows, :] = jnp.where(hi[rows, :] == h_b, lo[rows, :],
                                           jnp.full((PACK, tq), I16_MIN, I16))
        return carry

    lax.fori_loop(0, nck, mask_chunk, 0)
    l_thr, cnt_ge_l = greedy(lo_ref, n_sel - cnt_gt_h, cnt_ge_h - cnt_gt_h)
    n_ge = cnt_gt_h + cnt_ge_l
    bad = jnp.maximum(jnp.abs(n_ge - n_sel), jnp.where(h_thr == I16_MIN, 1.0, 0.0))
    exact = jnp.max(bad) == 0.0

    @pl.when(jnp.logical_not(exact))
    def _():
        cnt_gt_l = jnp.where(l_thr == I16_MAX, 0.0,
                             count_ge(lo_ref, jnp.minimum(l_thr + 1, I16_MAX)))
        need_ref[...] = n_sel - cnt_gt_h - cnt_gt_l
        run_ref[...] = jnp.zeros_like(run_ref)

    q = q_ref[...]
    qz = [q[(hd // 2) * LANES:(hd // 2 + 1) * LANES, :] * half_b[hd % 2]
          for hd in range(N_ATTN_HEADS)]
    take_all = l_thr == I16_MIN
    h_sel = jnp.broadcast_to(jnp.where(take_all, h_thr - 1, h_thr), (PACK, tq)).astype(I16)
    l_sel = jnp.broadcast_to(jnp.where(take_all, I16_MIN + 1, l_thr), (PACK, tq)).astype(I16)
    k_off_v = lax.broadcasted_iota(I32, (tkv, tq), 0)
    q_pos_v = i * tq + lax.broadcasted_iota(I32, (tkv, tq), 1)

    def make_bias(c):
        sub = pl.ds(pl.multiple_of((c % (tk // tkv)) * tkv, tkv), tkv)
        hi = hi_ref[c // (tk // tkv), sub, :]
        lo = lo_ref[c // (tk // tkv), sub, :]

        @pl.when(exact)
        def _():
            for r in range(tkv // PACK):
                rows = slice(r * PACK, (r + 1) * PACK)
                sel = (hi[rows, :] > h_sel) | (lo[rows, :] >= l_sel)
                bias_ref[rows, :] = jnp.where(sel, jnp.zeros((PACK, tq), BF16),
                                              jnp.full((PACK, tq), NEG_BIG, BF16)).astype(F32)

        @pl.when(jnp.logical_not(exact))
        def _():
            hi32 = hi.astype(I32)
            lo32 = lo.astype(I32)
            is_h = hi32 == h_thr
            eq = is_h & (lo32 == l_thr)
            gt = (hi32 > h_thr) | (is_h & (lo32 > l_thr))
            tri = (lax.broadcasted_iota(I32, (tkv, tkv), 1)
                   <= lax.broadcasted_iota(I32, (tkv, tkv), 0))
            prefix = jnp.dot(jnp.where(tri, 1.0, 0.0).astype(BF16),
                             jnp.where(eq, 1.0, 0.0).astype(BF16),
                             preferred_element_type=F32) + run_ref[...]
            take = (gt | (eq & (prefix <= need_ref[...]))) & (c * tkv + k_off_v <= q_pos_v)
            bias_ref[...] = jnp.where(take, 0.0, NEG_BIG)
            run_ref[...] += jnp.sum(jnp.where(eq, 1.0, 0.0), axis=0, keepdims=True)

    def attention(online):
        if online:
            m_ref[...] = jnp.full_like(m_ref, NEG_BIG)
        l_ref[...] = jnp.zeros_like(l_ref)
        acc_ref[...] = jnp.zeros_like(acc_ref)

        def att_chunk(c, carry):
            make_bias(c)
            start = pl.multiple_of(c * tkv, tkv)
            for hd in range(N_ATTN_HEADS):
                s = jnp.dot(k_ref[hd // 2, pl.ds(start, tkv), :], qz[hd],
                            preferred_element_type=F32) + bias_ref[...]
                if online:
                    m_old = m_ref[hd]
                    m_new = jnp.maximum(m_old, jnp.max(s, axis=0, keepdims=True))
                    alpha = jnp.exp2(m_old - m_new)
                    p = jnp.exp2(s - m_new)
                    l_ref[hd] = alpha * l_ref[hd] + jnp.sum(p, axis=0, keepdims=True)
                    acc_ref[hd] = alpha * acc_ref[hd]
                    m_ref[hd] = m_new
                else:
                    p = jnp.exp2(s)
                    l_ref[hd] += jnp.sum(p, axis=0, keepdims=True)
                p_ref[hd] = p.astype(BF16)
            for hd in range(N_ATTN_HEADS):
                acc_ref[hd] += jnp.dot(vt_ref[c, hd * HEAD_DIM:(hd + 1) * HEAD_DIM, :], p_ref[hd],
                                       preferred_element_type=F32)
            return carry

        lax.fori_loop(0, (i * tq) // tkv + 1, att_chunk, 0)

    bounded = flag_ref[0] != 0

    @pl.when(bounded)
    def _():
        attention(False)

    @pl.when(jnp.logical_not(bounded))
    def _():
        attention(True)

    for pair in range(N_ATTN_HEADS // 2):
        both = jnp.concatenate([acc_ref[2 * pair] / l_ref[2 * pair],
                                acc_ref[2 * pair + 1] / l_ref[2 * pair + 1]], axis=0)
        o_ref[:, pair * LANES:(pair + 1) * LANES] = both.T.astype(BF16)


def _dsa_call(flag, q, k, vt, iq, ik, iwt, *, tq, tk):
    b, _, s, _ = k.shape
    tkv = vt.shape[-1]
    n_sel = min(TOPK_MAX, s // 4)
    tile_t = lambda w: pl.BlockSpec((None, w, tq), lambda bi, i, f: (bi, 0, i))
    whole = lambda *shape: pl.BlockSpec((None,) + shape, lambda bi, i, f: (bi,) + (0,) * len(shape),
                                        pipeline_mode=pl.Buffered(1))
    grid_spec = pltpu.PrefetchScalarGridSpec(
        num_scalar_prefetch=1,
        grid=(b, s // tq),
        in_specs=[tile_t(ATTN_W), tile_t(IDX_W), tile_t(SUBLANES),
                  whole(N_ATTN_HEADS // 2, s, LANES), whole(s // tkv, ATTN_W, tkv),
                  whole(s, LANES)],
        out_specs=pl.BlockSpec((None, tq, ATTN_W), lambda bi, i, f: (bi, i, 0)),
        scratch_shapes=[
            pltpu.VMEM((s // tk, tk, tq), I16),
            pltpu.VMEM((s // tk, tk, tq), I16),
            pltpu.VMEM((tkv, tq), F32),
            pltpu.VMEM((N_ATTN_HEADS, tkv, tq), BF16),
            pltpu.VMEM((N_ATTN_HEADS, 1, tq), F32),
            pltpu.VMEM((N_ATTN_HEADS, 1, tq), F32),
            pltpu.VMEM((N_ATTN_HEADS, HEAD_DIM, tq), F32),
            pltpu.VMEM((1, tq), F32),
            pltpu.VMEM((1, tq), F32),
        ])
    return pl.pallas_call(
        functools.partial(_dsa_kernel, seq=s, tq=tq, tk=tk, tkv=tkv, n_sel=float(n_sel)),
        grid_spec=grid_spec,
        out_shape=jax.ShapeDtypeStruct((b, s, ATTN_W), BF16),
        compiler_params=pltpu.CompilerParams(
            dimension_semantics=("arbitrary", "arbitrary"), vmem_limit_bytes=VMEM_LIMIT_BYTES),
        name="dsa",
    )(flag, q, iq, iwt, k, vt, ik)


def _prep_w_in(w_in):
    offs = [sum(IN_SIZES[:n + 1]) for n in range(len(IN_SIZES) - 1)]
    aq, ak, av, iq, ik, iw, pu, mq = jnp.split(w_in, offs, axis=-1)
    pad = jnp.zeros(w_in.shape[:-1] + (LANES - N_IDX_HEADS,), w_in.dtype)
    return jnp.concatenate([aq, ak, av, iq, ik, ik, pu, mq, iw, pad], axis=-1).astype(BF16)


def _prep_ffn(gate, up, down, fc):
    depth, d, f = gate.shape
    split_cols = lambda w: w.astype(BF16).reshape(depth, d, f // fc, fc).transpose(0, 2, 1, 3)
    return split_cols(gate), split_cols(up), down.astype(BF16).reshape(depth, f // fc, fc, d)


def _block_diag(w_pool):
    depth, g, c, _ = w_pool.shape
    eye = jnp.eye(g, dtype=w_pool.dtype)
    return jnp.einsum("lgcd,gh->lgchd", w_pool, eye).reshape(depth, g * c, g * c).astype(BF16)


def kernel(x, mem, ffn1_norm, ffn1_gate, ffn1_up, ffn1_down, mix_norm, mem_norm, w_in, q_norm,
           k_norm, w_pool, pool_scale, w_mem_kv, mq_norm, mk_norm, w_out, ffn2_norm, ffn2_gate,
           ffn2_up, ffn2_down):
    b, s, d = x.shape
    depth = w_in.shape[0]
    tm = min(512, s)
    tq = min(256, s)
    tk = min(512, s)
    tkv = min(512, s)
    assert s % tm == 0 and s % tk == 0 and tk % tq == 0 and tk % tkv == 0 and tm % tkv == 0

    vec = lambda a, reps=1: jnp.tile(a, (1, reps))[:, None, :]
    ffn1 = _prep_ffn(ffn1_gate, ffn1_up, ffn1_down, MXU_N)
    ffn2 = _prep_ffn(ffn2_gate, ffn2_up, ffn2_down, MXU_N)
    w_in_p = _prep_w_in(w_in)
    w_out_b = w_out.astype(BF16)
    w_pool_bd = _block_diag(w_pool)
    head = jnp.arange(ATTN_W) // HEAD_DIM
    bd = jnp.where(head[:, None] == head[None, :], 1.0 / HEAD_DIM, 0.0).astype(BF16)
    qg, kg = vec(q_norm, N_ATTN_HEADS), vec(k_norm, N_ATTN_HEADS)
    mqg, mkg = vec(mq_norm, N_MEM_HEADS), vec(mk_norm, N_MEM_HEADS)
    f1g, f2g, mixg, memg, ps = vec(ffn1_norm), vec(ffn2_norm), vec(mix_norm), vec(mem_norm), vec(pool_scale)
    log2_bound = (NORM_SLACK * LOG2E * jnp.max(jnp.abs(q_norm), axis=-1)
                  * jnp.max(jnp.abs(k_norm), axis=-1))
    bounded = (log2_bound <= SAFE_LOG2_BOUND).astype(I32)

    mk, mv = _memkv_call(mem, memg, w_mem_kv.astype(BF16), mkg, bd)

    x2d = x.reshape(b * s, d)
    for l in range(depth):
        x2d = _ffn_call(x2d, l, f1g, *ffn1, tm=tm)
        q, k, vt, iq, ik, iwt, y_pm = _proj_call(
            x2d.reshape(b, s, d), l, mixg, w_in_p, qg, kg, mqg, bd, w_pool_bd, ps, mk, mv,
            tm=tm, tkv=tkv)
        y_attn = _dsa_call(bounded[l:l + 1], q, k, vt, iq, ik, iwt, tq=tq, tk=tk)
        x2d = _ffn_call(x2d, l, f2g, *ffn2, tm=tm,
                        mix=(y_attn.reshape(b * s, ATTN_W), y_pm.reshape(b * s, POOL_W + MEM_W),
                             w_out_b))
    return x2d.reshape(b, s, d)
```

```python
import functools
import math

import jax
import jax.numpy as jnp
from jax import lax
from jax.experimental import pallas as pl
from jax.experimental.pallas import tpu as pltpu

F32 = jnp.float32
BF16 = jnp.bfloat16
I32 = jnp.int32
I16 = jnp.int16

EPS = 1e-6
HEAD_DIM = 64
N_ATTN_HEADS = 8
N_IDX_HEADS = 4
N_POOL_GROUPS = 4
N_MEM_HEADS = 4
POOL_WINDOWS = (2, 4, 8, 16)
TOPK_MAX = 256

ATTN_W = N_ATTN_HEADS * HEAD_DIM
IDX_W = N_IDX_HEADS * HEAD_DIM
POOL_W = N_POOL_GROUPS * HEAD_DIM
MEM_W = N_MEM_HEADS * HEAD_DIM
IN_SIZES = (ATTN_W, ATTN_W, ATTN_W, IDX_W, HEAD_DIM, N_IDX_HEADS, POOL_W, MEM_W)

LANES = 128
SUBLANES = 8
PACK = 16
CNT_ROWS = 4 * PACK
MXU_N = 256
VMEM_LIMIT_BYTES = 62 * 1024 * 1024
INT_MIN = -(2 ** 31)
I16_MIN = -(2 ** 15)
I16_MAX = 2 ** 15 - 1
NEG_BIG = -1e30
LOG2E = math.log2(math.e)
NORM_SLACK = 8.1
SAFE_LOG2_BOUND = 80.0

OFF_Q, OFF_K, OFF_V = 0, ATTN_W, 2 * ATTN_W
OFF_IQ = 3 * ATTN_W
OFF_IK = OFF_IQ + IDX_W
OFF_PU = OFF_IK + LANES
OFF_MQ = OFF_PU + POOL_W
OFF_IW = OFF_MQ + MEM_W
D_IN_PAD = OFF_IW + LANES

NT_DIMS = (((1,), (1,)), ((), ()))


def _rms(x, g):
    return x * lax.rsqrt(jnp.mean(x * x, axis=-1, keepdims=True) + EPS) * g


def _head_rms(v, gain, bd):
    sq = v * v
    hi = sq.astype(BF16)
    lo = (sq - hi.astype(F32)).astype(BF16)
    ms = (jnp.dot(hi, bd, preferred_element_type=F32)
          + jnp.dot(lo, bd, preferred_element_type=F32))
    return v * lax.rsqrt(ms + EPS) * gain


def _half_masks(rows):
    lane = lax.broadcasted_iota(I32, (rows, LANES), 1)
    lo = lane < HEAD_DIM
    lo_b = jnp.where(lo, 1.0, 0.0).astype(BF16)
    hi_b = jnp.where(lo, 0.0, 1.0).astype(BF16)
    return lo, (lo_b, hi_b)


def _ffn_kernel(*refs, mixed, n_chunks):
    if mixed:
        (x_ref, ya_ref, yb_ref, wo_ref, g_ref, wg_ref, wu_ref, wd_ref,
         o_ref, h_ref, acc_ref) = refs
        x = (x_ref[...]
             + jnp.dot(ya_ref[...], wo_ref[:ATTN_W, :], preferred_element_type=F32)
             + jnp.dot(yb_ref[...], wo_ref[ATTN_W:, :], preferred_element_type=F32))
    else:
        x_ref, g_ref, wg_ref, wu_ref, wd_ref, o_ref, h_ref, acc_ref = refs
        x = x_ref[...]
    o_ref[...] = x
    h_ref[...] = _rms(x, g_ref[...]).astype(BF16)
    acc_ref[...] = jnp.zeros_like(acc_ref)

    def chunk(c, carry):
        h = h_ref[...]
        g = jnp.dot(h, wg_ref[c], preferred_element_type=F32)
        u = jnp.dot(h, wu_ref[c], preferred_element_type=F32)
        a = (g * (1.0 / (1.0 + jnp.exp(-g))) * u).astype(BF16)
        acc_ref[...] += jnp.dot(a, wd_ref[c], preferred_element_type=F32)
        return carry

    lax.fori_loop(0, n_chunks, chunk, 0, unroll=True)
    o_ref[...] += 0.5 * acc_ref[...]


def _ffn_call(x2d, layer, gain, wg, wu, wd, mix=None, *, tm):
    n, d = x2d.shape
    n_chunks, _, fc = wg.shape[1:]
    const = lambda *shape: pl.BlockSpec((None,) + shape, lambda i: (layer,) + (0,) * len(shape),
                                        pipeline_mode=pl.Buffered(1))
    row = lambda w: pl.BlockSpec((tm, w), lambda i: (i, 0))
    in_specs, args = [row(d)], [x2d]
    if mix is not None:
        ya, yb, wo = mix
        in_specs += [row(ya.shape[1]), row(yb.shape[1]), const(*wo.shape[1:])]
        args += [ya, yb, wo]
    in_specs += [const(1, d), const(n_chunks, d, fc), const(n_chunks, d, fc), const(n_chunks, fc, d)]
    args += [gain, wg, wu, wd]
    return pl.pallas_call(
        functools.partial(_ffn_kernel, mixed=mix is not None, n_chunks=n_chunks),
        grid=(n // tm,),
        in_specs=in_specs,
        out_specs=row(d),
        out_shape=jax.ShapeDtypeStruct((n, d), F32),
        scratch_shapes=[pltpu.VMEM((tm, d), BF16), pltpu.VMEM((tm, d), F32)],
        compiler_params=pltpu.CompilerParams(
            dimension_semantics=("arbitrary",), vmem_limit_bytes=VMEM_LIMIT_BYTES),
        name="ffn_mixed" if mix is not None else "ffn",
    )(*args)


def _memkv_kernel(mem_ref, g_ref, w_ref, kg_ref, bd_ref, mk_ref, mv_ref):
    mem_n = _rms(mem_ref[...], g_ref[...]).astype(BF16)
    kv = jnp.dot(mem_n, w_ref[...], preferred_element_type=F32)
    mk_ref[...] = _head_rms(kv[:, :MEM_W], kg_ref[...], bd_ref[:MEM_W, :MEM_W]).astype(BF16)
    mv_ref[...] = kv[:, MEM_W:].astype(BF16)


def _memkv_call(mem, mem_norm, w_kv, mk_gain, bd):
    b, m, d = mem.shape
    depth = w_kv.shape[0]
    out = jax.ShapeDtypeStruct((depth, b, m, MEM_W), BF16)
    out_spec = pl.BlockSpec((None, None, m, MEM_W), lambda l, i: (l, i, 0, 0))
    return pl.pallas_call(
        _memkv_kernel,
        grid=(depth, b),
        in_specs=[
            pl.BlockSpec((None, m, d), lambda l, i: (i, 0, 0)),
            pl.BlockSpec((None, 1, d), lambda l, i: (l, 0, 0)),
            pl.BlockSpec((None, d, 2 * MEM_W), lambda l, i: (l, 0, 0)),
            pl.BlockSpec((None, 1, MEM_W), lambda l, i: (l, 0, 0)),
            pl.BlockSpec(bd.shape, lambda l, i: (0, 0)),
        ],
        out_specs=[out_spec, out_spec],
        out_shape=[out, out],
        compiler_params=pltpu.CompilerParams(dimension_semantics=("arbitrary", "arbitrary")),
        name="memkv",
    )(mem, mem_norm, w_kv, mk_gain, bd)


POOL_HALO = 32


def _proj_kernel(x_ref, g_ref, w_ref, qg_ref, kg_ref, mqg_ref, bd_ref, wp_ref, ps_ref,
                 mk_ref, mv_ref,
                 q_ref, k_ref, vt_ref, iq_ref, ik_ref, iwt_ref, y_ref,
                 p_ref, s2_ref, s4_ref, s8_ref, *, tm, tkv):
    j = pl.program_id(1)
    rows = tm + POOL_HALO
    h = _rms(x_ref[...], g_ref[...]).astype(BF16)
    z = jnp.dot(h, w_ref[...], preferred_element_type=F32)
    bd = bd_ref[...]

    q_ref[...] = (_head_rms(z[:, OFF_Q:OFF_Q + ATTN_W], qg_ref[...], bd)
                  * (HEAD_DIM ** -0.5 * LOG2E)).T.astype(BF16)
    kn = _head_rms(z[:, OFF_K:OFF_K + ATTN_W], kg_ref[...], bd).astype(BF16)
    for pair in range(N_ATTN_HEADS // 2):
        k_ref[pair] = kn[:, pair * LANES:(pair + 1) * LANES]
    vt = z[:, OFF_V:OFF_V + ATTN_W].T
    for part in range(tm // tkv):
        vt_ref[part] = vt[:, part * tkv:(part + 1) * tkv].astype(BF16)
    iq_ref[...] = (z[:, OFF_IQ:OFF_IQ + IDX_W] * HEAD_DIM ** -0.5).T.astype(BF16)
    ik_ref[...] = z[:, OFF_IK:OFF_IK + LANES].astype(BF16)
    iwt_ref[...] = (z[:, OFF_IW:OFF_IW + LANES] * N_IDX_HEADS ** -0.5).T[0:SUBLANES, :]

    @pl.when(j == 0)
    def _():
        p_ref[0:POOL_HALO, :] = jnp.zeros((POOL_HALO, POOL_W), F32)

    pu = z[:, OFF_PU:OFF_PU + POOL_W]
    p_ref[POOL_HALO:rows, :] = pu
    s2_ref[8:rows, :] = p_ref[8:rows, :] + p_ref[7:rows - 1, :]
    s4_ref[16:rows, :] = s2_ref[16:rows, :] + s2_ref[14:rows - 2, :]
    s8_ref[24:rows, :] = s4_ref[24:rows, :] + s4_ref[20:rows - 4, :]
    s16 = s8_ref[POOL_HALO:rows, :] + s8_ref[24:rows - 8, :]
    lane = lax.broadcasted_iota(I32, (tm, POOL_W), 1)
    g1, g2, g3 = lane < HEAD_DIM, lane < 2 * HEAD_DIM, lane < 3 * HEAD_DIM
    wsum = jnp.where(g1, s2_ref[POOL_HALO:rows, :],
                     jnp.where(g2, s4_ref[POOL_HALO:rows, :],
                               jnp.where(g3, s8_ref[POOL_HALO:rows, :], s16)))
    win = jnp.where(g1, float(POOL_WINDOWS[0]),
                    jnp.where(g2, float(POOL_WINDOWS[1]),
                              jnp.where(g3, float(POOL_WINDOWS[2]), float(POOL_WINDOWS[3]))))
    t1 = (j * tm + 1 + lax.broadcasted_iota(I32, (tm, POOL_W), 0)).astype(F32)
    pooled = (wsum / jnp.minimum(t1, win) - pu).astype(BF16)
    y_pool = jnp.dot(pooled, wp_ref[...], preferred_element_type=F32) * ps_ref[...]
    y_ref[:, 0:POOL_W] = y_pool.astype(BF16)
    p_ref[16:POOL_HALO, :] = p_ref[tm + 16:rows, :]

    lo, half_b = _half_masks(tm)
    mqn = (_head_rms(z[:, OFF_MQ:OFF_MQ + MEM_W], mqg_ref[...], bd[:MEM_W, :MEM_W])
           * HEAD_DIM ** -0.5).astype(BF16)
    mk = mk_ref[...]
    mv = mv_ref[...]
    for pair in range(N_MEM_HEADS // 2):
        cols = slice(pair * LANES, (pair + 1) * LANES)
        outs = []
        for sub in range(2):
            logits = lax.dot_general(mqn[:, cols] * half_b[sub], mk[:, cols], NT_DIMS,
                                     preferred_element_type=F32)
            e = jnp.exp(logits - jnp.max(logits, axis=-1, keepdims=True))
            p = (e / jnp.sum(e, axis=-1, keepdims=True)).astype(BF16)
            outs.append(jnp.dot(p, mv[:, cols], preferred_element_type=F32))
        y_ref[:, POOL_W + pair * LANES:POOL_W + (pair + 1) * LANES] = (
            jnp.where(lo, outs[0], outs[1]).astype(BF16))


def _proj_call(x, layer, gain, w_in, qg, kg, mqg, bd, wp, ps, mk, mv, *, tm, tkv):
    b, s, d = x.shape
    m = mk.shape[2]
    lay = lambda *shape: pl.BlockSpec((None,) + shape, lambda i, j: (layer,) + (0,) * len(shape),
                                      pipeline_mode=pl.Buffered(1))
    tile = lambda w: pl.BlockSpec((None, tm, w), lambda i, j: (i, j, 0))
    mem = pl.BlockSpec((None, None, m, MEM_W), lambda i, j: (layer, i, 0, 0))
    sds = jax.ShapeDtypeStruct
    tile_t = lambda w: pl.BlockSpec((None, w, tm), lambda i, j: (i, 0, j))
    out_shape = [sds((b, ATTN_W, s), BF16),
                 sds((b, N_ATTN_HEADS // 2, s, LANES), BF16),
                 sds((b, s // tkv, ATTN_W, tkv), BF16),
                 sds((b, IDX_W, s), BF16),
                 sds((b, s, LANES), BF16),
                 sds((b, SUBLANES, s), F32),
                 sds((b, s, POOL_W + MEM_W), BF16)]
    out_specs = [tile_t(ATTN_W),
                 pl.BlockSpec((None, N_ATTN_HEADS // 2, tm, LANES), lambda i, j: (i, 0, j, 0)),
                 pl.BlockSpec((None, tm // tkv, ATTN_W, tkv), lambda i, j: (i, j, 0, 0)),
                 tile_t(IDX_W), tile(LANES), tile_t(SUBLANES),
                 tile(POOL_W + MEM_W)]
    rows = tm + POOL_HALO
    return pl.pallas_call(
        functools.partial(_proj_kernel, tm=tm, tkv=tkv),
        grid=(b, s // tm),
        in_specs=[tile(d), lay(1, d), lay(d, D_IN_PAD), lay(1, ATTN_W), lay(1, ATTN_W),
                  lay(1, MEM_W), pl.BlockSpec(bd.shape, lambda i, j: (0, 0)),
                  lay(POOL_W, POOL_W), lay(1, POOL_W), mem, mem],
        out_specs=out_specs,
        out_shape=out_shape,
        scratch_shapes=[pltpu.VMEM((rows, POOL_W), F32) for _ in range(4)],
        compiler_params=pltpu.CompilerParams(
            dimension_semantics=("arbitrary", "arbitrary"), vmem_limit_bytes=VMEM_LIMIT_BYTES),
        name="proj",
    )(x, gain, w_in, qg, kg, mqg, bd, wp, ps, mk, mv)


def _dsa_kernel(flag_ref, q_ref, iq_ref, iw_ref, k_ref, vt_ref, ik_ref, o_ref,
                hi_ref, lo_ref, bias_ref, p_ref, m_ref, l_ref, acc_ref, run_ref, need_ref,
                *, seq, tq, tk, tkv, n_sel):
    i = pl.program_id(1)
    nck = (i * tq) // tk + 1
    top = lax.broadcasted_iota(I32, (LANES, tq), 0) < HEAD_DIM
    half_b = (jnp.where(top, 1.0, 0.0).astype(BF16), jnp.where(top, 0.0, 1.0).astype(BF16))
    q_pos = i * tq + lax.broadcasted_iota(I32, (tk, tq), 1)
    k_off = lax.broadcasted_iota(I32, (tk, tq), 0)

    iq = iq_ref[...]
    iw = iw_ref[...]
    iqz = [iq[(hd // 2) * LANES:(hd // 2 + 1) * LANES, :] * half_b[hd % 2]
           for hd in range(N_IDX_HEADS)]
    wrow = [iw[hd:hd + 1, :] for hd in range(N_IDX_HEADS)]

    def score_chunk(c, causal):
        ikc = ik_ref[pl.ds(pl.multiple_of(c * tk, tk), tk), :]
        score = None
        for hd in range(N_IDX_HEADS):
            d = jnp.dot(ikc, iqz[hd], preferred_element_type=F32)
            t = jnp.maximum(d, 0.0) * wrow[hd]
            score = t if score is None else score + t
        bits = pltpu.bitcast(score, I32)
        neg = bits >> 31
        key = (bits ^ (neg & 0x7FFFFFFF)) - neg
        key = jnp.where(key == 0, (seq - 1 - c * tk) - k_off, key + (~neg & seq))
        if causal:
            key = jnp.where(c * tk + k_off <= q_pos, key, INT_MIN)
        hi_ref[c] = (key >> 16).astype(I16)
        lo_ref[c] = (key ^ 0x8000).astype(I16)
        return 0

    lax.fori_loop(0, nck - 1, lambda c, _: score_chunk(c, False), 0)
    score_chunk(nck - 1, True)

    def count_ge(ref, cand):
        cand_b = jnp.broadcast_to(cand, (CNT_ROWS, tq)).astype(I16)
        one = jnp.ones((CNT_ROWS, tq), I16)
        zero = jnp.zeros((CNT_ROWS, tq), I16)

        def cnt_chunk(c, acc):
            a = ref[c]
            for r in range(tk // CNT_ROWS):
                acc = acc + jnp.where(a[r * CNT_ROWS:(r + 1) * CNT_ROWS, :] >= cand_b, one, zero)
            return acc

        acc = lax.fori_loop(0, nck, cnt_chunk, zero)
        return jnp.sum(acc.astype(I32).astype(F32), axis=0, keepdims=True)

    def greedy(ref, target, cnt0):
        def step(it, carry):
            t_u, cnt_t = carry
            cand_u = t_u | lax.shift_left(jnp.int32(1), 15 - it)
            cnt = count_ge(ref, cand_u + I16_MIN)
            ok = cnt >= target
            return jnp.where(ok, cand_u, t_u), jnp.where(ok, cnt, cnt_t)
        t_u, cnt_t = lax.fori_loop(0, 16, step, (jnp.zeros((1, tq), I32), cnt0))
        return t_u + I16_MIN, cnt_t

    total = jnp.full((1, tq), 1.0, F32) * (nck * tk).astype(F32)
    h_thr, cnt_ge_h = greedy(hi_ref, n_sel, total)
    cnt_gt_h = jnp.where(h_thr == I16_MAX, 0.0,
                         count_ge(hi_ref, jnp.minimum(h_thr + 1, I16_MAX)))
    h_b = jnp.broadcast_to(h_thr, (PACK, tq)).astype(I16)

    def mask_chunk(c, carry):
        hi = hi_ref[c]
        lo = lo_ref[c]
        for r in range(tk // PACK):
            rows = slice(r * PACK, (r + 1) * PACK)
            lo_ref[c, rows, :] = jnp.where(hi[rows, :] == h_b, lo[rows, :],
                                           jnp.full((PACK, tq), I16_MIN, I16))
        return carry

    lax.fori_loop(0, nck, mask_chunk, 0)
    l_thr, cnt_ge_l = greedy(lo_ref, n_sel - cnt_gt_h, cnt_ge_h - cnt_gt_h)
    n_ge = cnt_gt_h + cnt_ge_l
    bad = jnp.maximum(jnp.abs(n_ge - n_sel), jnp.where(h_thr == I16_MIN, 1.0, 0.0))
    exact = jnp.max(bad) == 0.0

    @pl.when(jnp.logical_not(exact))
    def _():
        cnt_gt_l = jnp.where(l_thr == I16_MAX, 0.0,
                             count_ge(lo_ref, jnp.minimum(l_thr + 1, I16_MAX)))
        need_ref[...] = n_sel - cnt_gt_h - cnt_gt_l
        run_ref[...] = jnp.zeros_like(run_ref)

    q = q_ref[...]
    qz = [q[(hd // 2) * LANES:(hd // 2 + 1) * LANES, :] * half_b[hd % 2]
          for hd in range(N_ATTN_HEADS)]
    take_all = l_thr == I16_MIN
    h_sel = jnp.broadcast_to(jnp.where(take_all, h_thr - 1, h_thr), (PACK, tq)).astype(I16)
    l_sel = jnp.broadcast_to(jnp.where(take_all, I16_MIN + 1, l_thr), (PACK, tq)).astype(I16)
    k_off_v = lax.broadcasted_iota(I32, (tkv, tq), 0)
    q_pos_v = i * tq + lax.broadcasted_iota(I32, (tkv, tq), 1)

    def make_bias(c, with_ties):
        sub = pl.ds(pl.multiple_of((c % (tk // tkv)) * tkv, tkv), tkv)
        hi = hi_ref[c // (tk // tkv), sub, :]
        lo = lo_ref[c // (tk // tkv), sub, :]

        def threshold_mask():
            for r in range(tkv // PACK):
                rows = slice(r * PACK, (r + 1) * PACK)
                sel = (hi[rows, :] > h_sel) | (lo[rows, :] >= l_sel)
                bias_ref[rows, :] = jnp.where(sel, jnp.zeros((PACK, tq), BF16),
                                              jnp.full((PACK, tq), NEG_BIG, BF16)).astype(F32)

        if not with_ties:
            threshold_mask()
            return
        pl.when(exact)(threshold_mask)

        @pl.when(jnp.logical_not(exact))
        def _():
            hi32 = hi.astype(I32)
            lo32 = lo.astype(I32)
            is_h = hi32 == h_thr
            eq = is_h & (lo32 == l_thr)
            gt = (hi32 > h_thr) | (is_h & (lo32 > l_thr))
            tri = (lax.broadcasted_iota(I32, (tkv, tkv), 1)
                   <= lax.broadcasted_iota(I32, (tkv, tkv), 0))
            prefix = jnp.dot(jnp.where(tri, 1.0, 0.0).astype(BF16),
                             jnp.where(eq, 1.0, 0.0).astype(BF16),
                             preferred_element_type=F32) + run_ref[...]
            take = (gt | (eq & (prefix <= need_ref[...]))) & (c * tkv + k_off_v <= q_pos_v)
            bias_ref[...] = jnp.where(take, 0.0, NEG_BIG)
            run_ref[...] += jnp.sum(jnp.where(eq, 1.0, 0.0), axis=0, keepdims=True)

    def attention(online, skewed):
        if online:
            m_ref[...] = jnp.full_like(m_ref, NEG_BIG)
        l_ref[...] = jnp.zeros_like(l_ref)
        acc_ref[...] = jnp.zeros_like(acc_ref)

        def probs(c, slot):
            make_bias(c, with_ties=not skewed)
            start = pl.multiple_of(c * tkv, tkv)
            for hd in range(N_ATTN_HEADS):
                s = jnp.dot(k_ref[hd // 2, pl.ds(start, tkv), :], qz[hd],
                            preferred_element_type=F32) + bias_ref[...]
                if online:
                    m_old = m_ref[hd]
                    m_new = jnp.maximum(m_old, jnp.max(s, axis=0, keepdims=True))
                    alpha = jnp.exp2(m_old - m_new)
                    p = jnp.exp2(s - m_new)
                    l_ref[hd] = alpha * l_ref[hd] + jnp.sum(p, axis=0, keepdims=True)
                    acc_ref[hd] = alpha * acc_ref[hd]
                    m_ref[hd] = m_new
                else:
                    p = jnp.exp2(s)
                    l_ref[hd] += jnp.sum(p, axis=0, keepdims=True)
                p_ref[slot, hd] = p.astype(BF16)

        def weigh_values(c, slot):
            for hd in range(N_ATTN_HEADS):
                acc_ref[hd] += jnp.dot(vt_ref[c, hd * HEAD_DIM:(hd + 1) * HEAD_DIM, :],
                                       p_ref[slot, hd], preferred_element_type=F32)

        n_att = (i * tq) // tkv + 1
        if not skewed:
            def att_chunk(c, carry):
                probs(c, 0)
                weigh_values(c, 0)
                return carry

            lax.fori_loop(0, n_att, att_chunk, 0)
        else:
            def att_pair(j, carry):
                probs(2 * j + 1, 1)
                weigh_values(2 * j, 0)
                probs(2 * j + 2, 0)
                weigh_values(2 * j + 1, 1)
                return carry

            probs(0, 0)
            n_pairs = (n_att - 1) // 2
            lax.fori_loop(0, n_pairs, att_pair, 0)
            last = 2 * n_pairs

            @pl.when(last + 1 < n_att)
            def _():
                probs(last + 1, 1)
                weigh_values(last, 0)
                weigh_values(last + 1, 1)

            @pl.when(last + 1 == n_att)
            def _():
                weigh_values(last, 0)

    bounded = flag_ref[0] != 0

    @pl.when(bounded & exact)
    def _():
        attention(online=False, skewed=True)

    @pl.when(bounded & jnp.logical_not(exact))
    def _():
        attention(online=False, skewed=False)

    @pl.when(jnp.logical_not(bounded))
    def _():
        attention(online=True, skewed=False)

    for pair in range(N_ATTN_HEADS // 2):
        both = jnp.concatenate([acc_ref[2 * pair] / l_ref[2 * pair],
                                acc_ref[2 * pair + 1] / l_ref[2 * pair + 1]], axis=0)
        o_ref[:, pair * LANES:(pair + 1) * LANES] = both.T.astype(BF16)


def _dsa_call(flag, q, k, vt, iq, ik, iwt, *, tq, tk):
    b, _, s, _ = k.shape
    tkv = vt.shape[-1]
    n_sel = min(TOPK_MAX, s // 4)
    tile_t = lambda w: pl.BlockSpec((None, w, tq), lambda bi, i, f: (bi, 0, i))
    whole = lambda *shape: pl.BlockSpec((None,) + shape, lambda bi, i, f: (bi,) + (0,) * len(shape),
                                        pipeline_mode=pl.Buffered(1))
    grid_spec = pltpu.PrefetchScalarGridSpec(
        num_scalar_prefetch=1,
        grid=(b, s // tq),
        in_specs=[tile_t(ATTN_W), tile_t(IDX_W), tile_t(SUBLANES),
                  whole(N_ATTN_HEADS // 2, s, LANES), whole(s // tkv, ATTN_W, tkv),
                  whole(s, LANES)],
        out_specs=pl.BlockSpec((None, tq, ATTN_W), lambda bi, i, f: (bi, i, 0)),
        scratch_shapes=[
            pltpu.VMEM((s // tk, tk, tq), I16),
            pltpu.VMEM((s // tk, tk, tq), I16),
            pltpu.VMEM((tkv, tq), F32),
            pltpu.VMEM((2, N_ATTN_HEADS, tkv, tq), BF16),
            pltpu.VMEM((N_ATTN_HEADS, 1, tq), F32),
            pltpu.VMEM((N_ATTN_HEADS, 1, tq), F32),
            pltpu.VMEM((N_ATTN_HEADS, HEAD_DIM, tq), F32),
            pltpu.VMEM((1, tq), F32),
            pltpu.VMEM((1, tq), F32),
        ])
    return pl.pallas_call(
        functools.partial(_dsa_kernel, seq=s, tq=tq, tk=tk, tkv=tkv, n_sel=float(n_sel)),
        grid_spec=grid_spec,
        out_shape=jax.ShapeDtypeStruct((b, s, ATTN_W), BF16),
        compiler_params=pltpu.CompilerParams(
            dimension_semantics=("arbitrary", "arbitrary"), vmem_limit_bytes=VMEM_LIMIT_BYTES),
        name="dsa",
    )(flag, q, iq, iwt, k, vt, ik)


def _prep_w_in(w_in):
    offs = [sum(IN_SIZES[:n + 1]) for n in range(len(IN_SIZES) - 1)]
    aq, ak, av, iq, ik, iw, pu, mq = jnp.split(w_in, offs, axis=-1)
    pad = jnp.zeros(w_in.shape[:-1] + (LANES - N_IDX_HEADS,), w_in.dtype)
    return jnp.concatenate([aq, ak, av, iq, ik, ik, pu, mq, iw, pad], axis=-1).astype(BF16)


def _prep_ffn(gate, up, down, fc):
    depth, d, f = gate.shape
    split_cols = lambda w: w.astype(BF16).reshape(depth, d, f // fc, fc).transpose(0, 2, 1, 3)
    return split_cols(gate), split_cols(up), down.astype(BF16).reshape(depth, f // fc, fc, d)


def _block_diag(w_pool):
    depth, g, c, _ = w_pool.shape
    eye = jnp.eye(g, dtype=w_pool.dtype)
    return jnp.einsum("lgcd,gh->lgchd", w_pool, eye).reshape(depth, g * c, g * c).astype(BF16)


def kernel(x, mem, ffn1_norm, ffn1_gate, ffn1_up, ffn1_down, mix_norm, mem_norm, w_in, q_norm,
           k_norm, w_pool, pool_scale, w_mem_kv, mq_norm, mk_norm, w_out, ffn2_norm, ffn2_gate,
           ffn2_up, ffn2_down):
    b, s, d = x.shape
    depth = w_in.shape[0]
    tm = min(512, s)
    tq = min(256, s)
    tk = min(512, s)
    tkv = min(512, s)
    assert s % tm == 0 and s % tk == 0 and tk % tq == 0 and tk % tkv == 0 and tm % tkv == 0

    vec = lambda a, reps=1: jnp.tile(a, (1, reps))[:, None, :]
    ffn1 = _prep_ffn(ffn1_gate, ffn1_up, ffn1_down, MXU_N)
    ffn2 = _prep_ffn(ffn2_gate, ffn2_up, ffn2_down, MXU_N)
    w_in_p = _prep_w_in(w_in)
    w_out_b = w_out.astype(BF16)
    w_pool_bd = _block_diag(w_pool)
    head = jnp.arange(ATTN_W) // HEAD_DIM
    bd = jnp.where(head[:, None] == head[None, :], 1.0 / HEAD_DIM, 0.0).astype(BF16)
    qg, kg = vec(q_norm, N_ATTN_HEADS), vec(k_norm, N_ATTN_HEADS)
    mqg, mkg = vec(mq_norm, N_MEM_HEADS), vec(mk_norm, N_MEM_HEADS)
    f1g, f2g, mixg, memg, ps = vec(ffn1_norm), vec(ffn2_norm), vec(mix_norm), vec(mem_norm), vec(pool_scale)
    log2_bound = (NORM_SLACK * LOG2E * jnp.max(jnp.abs(q_norm), axis=-1)
                  * jnp.max(jnp.abs(k_norm), axis=-1))
    bounded = (log2_bound <= SAFE_LOG2_BOUND).astype(I32)

    mk, mv = _memkv_call(mem, memg, w_mem_kv.astype(BF16), mkg, bd)

    x2d = x.reshape(b * s, d)
    for l in range(depth):
        x2d = _ffn_call(x2d, l, f1g, *ffn1, tm=tm)
        q, k, vt, iq, ik, iwt, y_pm = _proj_call(
            x2d.reshape(b, s, d), l, mixg, w_in_p, qg, kg, mqg, bd, w_pool_bd, ps, mk, mv,
            tm=tm, tkv=tkv)
        y_attn = _dsa_call(bounded[l:l + 1], q, k, vt, iq, ik, iwt, tq=tq, tk=tk)
        x2d = _ffn_call(x2d, l, f2g, *ffn2, tm=tm,
                        mix=(y_attn.reshape(b * s, ATTN_W), y_pm.reshape(b * s, POOL_W + MEM_W),
                             w_out_b))
    return x2d.reshape(b, s, d)
```

```python
import functools
import math

import jax
import jax.numpy as jnp
from jax import lax
from jax.experimental import pallas as pl
from jax.experimental.pallas import tpu as pltpu

F32 = jnp.float32
BF16 = jnp.bfloat16
I32 = jnp.int32
I16 = jnp.int16

EPS = 1e-6
HEAD_DIM = 64
N_ATTN_HEADS = 8
N_IDX_HEADS = 4
N_POOL_GROUPS = 4
N_MEM_HEADS = 4
POOL_WINDOWS = (2, 4, 8, 16)
TOPK_MAX = 256

ATTN_W = N_ATTN_HEADS * HEAD_DIM
IDX_W = N_IDX_HEADS * HEAD_DIM
POOL_W = N_POOL_GROUPS * HEAD_DIM
MEM_W = N_MEM_HEADS * HEAD_DIM
IN_SIZES = (ATTN_W, ATTN_W, ATTN_W, IDX_W, HEAD_DIM, N_IDX_HEADS, POOL_W, MEM_W)

LANES = 128
SUBLANES = 8
PACK = 16
CNT_ROWS = 4 * PACK
MXU_N = 256
VMEM_LIMIT_BYTES = 62 * 1024 * 1024
INT_MIN = -(2 ** 31)
I16_MIN = -(2 ** 15)
I16_MAX = 2 ** 15 - 1
NEG_BIG = -1e30
LOG2E = math.log2(math.e)
NORM_SLACK = 8.1
SAFE_LOG2_BOUND = 80.0

OFF_Q, OFF_K, OFF_V = 0, ATTN_W, 2 * ATTN_W
OFF_IQ = 3 * ATTN_W
OFF_IK = OFF_IQ + IDX_W
OFF_PU = OFF_IK + LANES
OFF_MQ = OFF_PU + POOL_W
OFF_IW = OFF_MQ + MEM_W
D_IN_PAD = OFF_IW + LANES

NT_DIMS = (((1,), (1,)), ((), ()))


def _rms(x, g):
    return x * lax.rsqrt(jnp.mean(x * x, axis=-1, keepdims=True) + EPS) * g


def _head_rms(v, gain, bd):
    sq = v * v
    hi = sq.astype(BF16)
    lo = (sq - hi.astype(F32)).astype(BF16)
    ms = (jnp.dot(hi, bd, preferred_element_type=F32)
          + jnp.dot(lo, bd, preferred_element_type=F32))
    return v * lax.rsqrt(ms + EPS) * gain


def _half_masks(rows):
    lane = lax.broadcasted_iota(I32, (rows, LANES), 1)
    lo = lane < HEAD_DIM
    lo_b = jnp.where(lo, 1.0, 0.0).astype(BF16)
    hi_b = jnp.where(lo, 0.0, 1.0).astype(BF16)
    return lo, (lo_b, hi_b)


def _ffn_kernel(*refs, mixed, n_chunks):
    if mixed:
        (x_ref, ya_ref, yb_ref, wo_ref, g_ref, wg_ref, wu_ref, wd_ref,
         o_ref, h_ref, acc_ref) = refs
        x = (x_ref[...]
             + jnp.dot(ya_ref[...], wo_ref[:ATTN_W, :], preferred_element_type=F32)
             + jnp.dot(yb_ref[...], wo_ref[ATTN_W:, :], preferred_element_type=F32))
    else:
        x_ref, g_ref, wg_ref, wu_ref, wd_ref, o_ref, h_ref, acc_ref = refs
        x = x_ref[...]
    o_ref[...] = x
    h_ref[...] = _rms(x, g_ref[...]).astype(BF16)
    acc_ref[...] = jnp.zeros_like(acc_ref)

    def chunk(c, carry):
        h = h_ref[...]
        g = jnp.dot(h, wg_ref[c], preferred_element_type=F32)
        u = jnp.dot(h, wu_ref[c], preferred_element_type=F32)
        a = (g * (1.0 / (1.0 + jnp.exp(-g))) * u).astype(BF16)
        acc_ref[...] += jnp.dot(a, wd_ref[c], preferred_element_type=F32)
        return carry

    lax.fori_loop(0, n_chunks, chunk, 0, unroll=True)
    o_ref[...] += 0.5 * acc_ref[...]


def _ffn_call(x2d, layer, gain, wg, wu, wd, mix=None, *, tm):
    n, d = x2d.shape
    n_chunks, _, fc = wg.shape[1:]
    const = lambda *shape: pl.BlockSpec((None,) + shape, lambda i: (layer,) + (0,) * len(shape),
                                        pipeline_mode=pl.Buffered(1))
    row = lambda w: pl.BlockSpec((tm, w), lambda i: (i, 0))
    in_specs, args = [row(d)], [x2d]
    if mix is not None:
        ya, yb, wo = mix
        in_specs += [row(ya.shape[1]), row(yb.shape[1]), const(*wo.shape[1:])]
        args += [ya, yb, wo]
    in_specs += [const(1, d), const(n_chunks, d, fc), const(n_chunks, d, fc), const(n_chunks, fc, d)]
    args += [gain, wg, wu, wd]
    return pl.pallas_call(
        functools.partial(_ffn_kernel, mixed=mix is not None, n_chunks=n_chunks),
        grid=(n // tm,),
        in_specs=in_specs,
        out_specs=row(d),
        out_shape=jax.ShapeDtypeStruct((n, d), F32),
        scratch_shapes=[pltpu.VMEM((tm, d), BF16), pltpu.VMEM((tm, d), F32)],
        compiler_params=pltpu.CompilerParams(
            dimension_semantics=("arbitrary",), vmem_limit_bytes=VMEM_LIMIT_BYTES),
        name="ffn_mixed" if mix is not None else "ffn",
    )(*args)


def _memkv_kernel(mem_ref, g_ref, w_ref, kg_ref, bd_ref, mk_ref, mv_ref):
    mem_n = _rms(mem_ref[...], g_ref[...]).astype(BF16)
    kv = jnp.dot(mem_n, w_ref[...], preferred_element_type=F32)
    mk_ref[...] = _head_rms(kv[:, :MEM_W], kg_ref[...], bd_ref[:MEM_W, :MEM_W]).astype(BF16)
    mv_ref[...] = kv[:, MEM_W:].astype(BF16)


def _memkv_call(mem, mem_norm, w_kv, mk_gain, bd):
    b, m, d = mem.shape
    depth = w_kv.shape[0]
    out = jax.ShapeDtypeStruct((depth, b, m, MEM_W), BF16)
    out_spec = pl.BlockSpec((None, None, m, MEM_W), lambda l, i: (l, i, 0, 0))
    return pl.pallas_call(
        _memkv_kernel,
        grid=(depth, b),
        in_specs=[
            pl.BlockSpec((None, m, d), lambda l, i: (i, 0, 0)),
            pl.BlockSpec((None, 1, d), lambda l, i: (l, 0, 0)),
            pl.BlockSpec((None, d, 2 * MEM_W), lambda l, i: (l, 0, 0)),
            pl.BlockSpec((None, 1, MEM_W), lambda l, i: (l, 0, 0)),
            pl.BlockSpec(bd.shape, lambda l, i: (0, 0)),
        ],
        out_specs=[out_spec, out_spec],
        out_shape=[out, out],
        compiler_params=pltpu.CompilerParams(dimension_semantics=("arbitrary", "arbitrary")),
        name="memkv",
    )(mem, mem_norm, w_kv, mk_gain, bd)


POOL_HALO = 32


def _proj_kernel(x_ref, g_ref, w_ref, qg_ref, kg_ref, mqg_ref, bd_ref, wp_ref, ps_ref,
                 mk_ref, mv_ref,
                 q_ref, k_ref, vt_ref, iq_ref, ik_ref, iwt_ref, y_ref,
                 p_ref, s2_ref, s4_ref, s8_ref, *, tm, tkv):
    j = pl.program_id(1)
    rows = tm + POOL_HALO
    h = _rms(x_ref[...], g_ref[...]).astype(BF16)
    z = jnp.dot(h, w_ref[...], preferred_element_type=F32)
    bd = bd_ref[...]

    q_ref[...] = (_head_rms(z[:, OFF_Q:OFF_Q + ATTN_W], qg_ref[...], bd)
                  * (HEAD_DIM ** -0.5 * LOG2E)).T.astype(BF16)
    kn = _head_rms(z[:, OFF_K:OFF_K + ATTN_W], kg_ref[...], bd).astype(BF16)
    for pair in range(N_ATTN_HEADS // 2):
        k_ref[pair] = kn[:, pair * LANES:(pair + 1) * LANES]
    vt = z[:, OFF_V:OFF_V + ATTN_W].T
    for part in range(tm // tkv):
        vt_ref[part] = vt[:, part * tkv:(part + 1) * tkv].astype(BF16)
    iq_ref[...] = (z[:, OFF_IQ:OFF_IQ + IDX_W] * HEAD_DIM ** -0.5).T.astype(BF16)
    ik_ref[...] = z[:, OFF_IK:OFF_IK + LANES].astype(BF16)
    iwt_ref[...] = (z[:, OFF_IW:OFF_IW + LANES] * N_IDX_HEADS ** -0.5).T[0:SUBLANES, :]

    @pl.when(j == 0)
    def _():
        p_ref[0:POOL_HALO, :] = jnp.zeros((POOL_HALO, POOL_W), F32)

    pu = z[:, OFF_PU:OFF_PU + POOL_W]
    p_ref[POOL_HALO:rows, :] = pu
    s2_ref[8:rows, :] = p_ref[8:rows, :] + p_ref[7:rows - 1, :]
    s4_ref[16:rows, :] = s2_ref[16:rows, :] + s2_ref[14:rows - 2, :]
    s8_ref[24:rows, :] = s4_ref[24:rows, :] + s4_ref[20:rows - 4, :]
    s16 = s8_ref[POOL_HALO:rows, :] + s8_ref[24:rows - 8, :]
    lane = lax.broadcasted_iota(I32, (tm, POOL_W), 1)
    g1, g2, g3 = lane < HEAD_DIM, lane < 2 * HEAD_DIM, lane < 3 * HEAD_DIM
    wsum = jnp.where(g1, s2_ref[POOL_HALO:rows, :],
                     jnp.where(g2, s4_ref[POOL_HALO:rows, :],
                               jnp.where(g3, s8_ref[POOL_HALO:rows, :], s16)))
    win = jnp.where(g1, float(POOL_WINDOWS[0]),
                    jnp.where(g2, float(POOL_WINDOWS[1]),
                              jnp.where(g3, float(POOL_WINDOWS[2]), float(POOL_WINDOWS[3]))))
    t1 = (j * tm + 1 + lax.broadcasted_iota(I32, (tm, POOL_W), 0)).astype(F32)
    pooled = (wsum / jnp.minimum(t1, win) - pu).astype(BF16)
    y_pool = jnp.dot(pooled, wp_ref[...], preferred_element_type=F32) * ps_ref[...]
    y_ref[:, 0:POOL_W] = y_pool.astype(BF16)
    p_ref[16:POOL_HALO, :] = p_ref[tm + 16:rows, :]

    lo, half_b = _half_masks(tm)
    mqn = (_head_rms(z[:, OFF_MQ:OFF_MQ + MEM_W], mqg_ref[...], bd[:MEM_W, :MEM_W])
           * HEAD_DIM ** -0.5).astype(BF16)
    mk = mk_ref[...]
    mv = mv_ref[...]
    for pair in range(N_MEM_HEADS // 2):
        cols = slice(pair * LANES, (pair + 1) * LANES)
        outs = []
        for sub in range(2):
            logits = lax.dot_general(mqn[:, cols] * half_b[sub], mk[:, cols], NT_DIMS,
                                     preferred_element_type=F32)
            e = jnp.exp(logits - jnp.max(logits, axis=-1, keepdims=True))
            p = (e / jnp.sum(e, axis=-1, keepdims=True)).astype(BF16)
            outs.append(jnp.dot(p, mv[:, cols], preferred_element_type=F32))
        y_ref[:, POOL_W + pair * LANES:POOL_W + (pair + 1) * LANES] = (
            jnp.where(lo, outs[0], outs[1]).astype(BF16))


def _proj_call(x, layer, gain, w_in, qg, kg, mqg, bd, wp, ps, mk, mv, *, tm, tkv):
    b, s, d = x.shape
    m = mk.shape[2]
    lay = lambda *shape: pl.BlockSpec((None,) + shape, lambda i, j: (layer,) + (0,) * len(shape),
                                      pipeline_mode=pl.Buffered(1))
    tile = lambda w: pl.BlockSpec((None, tm, w), lambda i, j: (i, j, 0))
    mem = pl.BlockSpec((None, None, m, MEM_W), lambda i, j: (layer, i, 0, 0))
    sds = jax.ShapeDtypeStruct
    tile_t = lambda w: pl.BlockSpec((None, w, tm), lambda i, j: (i, 0, j))
    out_shape = [sds((b, ATTN_W, s), BF16),
                 sds((b, N_ATTN_HEADS // 2, s, LANES), BF16),
                 sds((b, s // tkv, ATTN_W, tkv), BF16),
                 sds((b, IDX_W, s), BF16),
                 sds((b, s, LANES), BF16),
                 sds((b, SUBLANES, s), F32),
                 sds((b, s, POOL_W + MEM_W), BF16)]
    out_specs = [tile_t(ATTN_W),
                 pl.BlockSpec((None, N_ATTN_HEADS // 2, tm, LANES), lambda i, j: (i, 0, j, 0)),
                 pl.BlockSpec((None, tm // tkv, ATTN_W, tkv), lambda i, j: (i, j, 0, 0)),
                 tile_t(IDX_W), tile(LANES), tile_t(SUBLANES),
                 tile(POOL_W + MEM_W)]
    rows = tm + POOL_HALO
    return pl.pallas_call(
        functools.partial(_proj_kernel, tm=tm, tkv=tkv),
        grid=(b, s // tm),
        in_specs=[tile(d), lay(1, d), lay(d, D_IN_PAD), lay(1, ATTN_W), lay(1, ATTN_W),
                  lay(1, MEM_W), pl.BlockSpec(bd.shape, lambda i, j: (0, 0)),
                  lay(POOL_W, POOL_W), lay(1, POOL_W), mem, mem],
        out_specs=out_specs,
        out_shape=out_shape,
        scratch_shapes=[pltpu.VMEM((rows, POOL_W), F32) for _ in range(4)],
        compiler_params=pltpu.CompilerParams(
            dimension_semantics=("arbitrary", "arbitrary"), vmem_limit_bytes=VMEM_LIMIT_BYTES),
        name="proj",
    )(x, gain, w_in, qg, kg, mqg, bd, wp, ps, mk, mv)


def _dsa_kernel(flag_ref, q_ref, iq_ref, iw_ref, k_ref, vt_ref, ik_ref, o_ref,
                hi_ref, lo_ref, bias_ref, p_ref, m_ref, l_ref, acc_ref, run_ref, need_ref,
                *, seq, tq, tk, tkv, n_sel):
    i = pl.program_id(1)
    nck = (i * tq) // tk + 1
    top = lax.broadcasted_iota(I32, (LANES, tq), 0) < HEAD_DIM
    half_b = (jnp.where(top, 1.0, 0.0).astype(BF16), jnp.where(top, 0.0, 1.0).astype(BF16))
    q_pos = i * tq + lax.broadcasted_iota(I32, (tk, tq), 1)
    k_off = lax.broadcasted_iota(I32, (tk, tq), 0)

    iq = iq_ref[...]
    iw = iw_ref[...]
    iqz = [iq[(hd // 2) * LANES:(hd // 2 + 1) * LANES, :] * half_b[hd % 2]
           for hd in range(N_IDX_HEADS)]
    wrow = [iw[hd:hd + 1, :] for hd in range(N_IDX_HEADS)]

    def score_chunk(c, causal):
        ikc = ik_ref[pl.ds(pl.multiple_of(c * tk, tk), tk), :]
        score = None
        for hd in range(N_IDX_HEADS):
            d = jnp.dot(ikc, iqz[hd], preferred_element_type=F32)
            t = jnp.maximum(d, 0.0) * wrow[hd]
            score = t if score is None else score + t
        bits = pltpu.bitcast(score, I32)
        neg = bits >> 31
        key = (bits ^ (neg & 0x7FFFFFFF)) - neg
        key = jnp.where(key == 0, (seq - 1 - c * tk) - k_off, key + (~neg & seq))
        if causal:
            key = jnp.where(c * tk + k_off <= q_pos, key, INT_MIN)
        hi_ref[c] = (key >> 16).astype(I16)
        lo_ref[c] = (key ^ 0x8000).astype(I16)
        return 0

    def score_pair(j, carry):
        score_chunk(2 * j, False)
        score_chunk(2 * j + 1, False)
        return carry

    lax.fori_loop(0, (nck - 1) // 2, score_pair, 0)

    @pl.when((nck - 1) % 2 == 1)
    def _():
        score_chunk(nck - 2, False)

    score_chunk(nck - 1, True)

    def count_ge(ref, cand):
        cand_b = jnp.broadcast_to(cand, (CNT_ROWS, tq)).astype(I16)
        one = jnp.ones((CNT_ROWS, tq), I16)
        zero = jnp.zeros((CNT_ROWS, tq), I16)

        def cnt_chunk(c, acc):
            a = ref[c]
            for r in range(tk // CNT_ROWS):
                acc = acc + jnp.where(a[r * CNT_ROWS:(r + 1) * CNT_ROWS, :] >= cand_b, one, zero)
            return acc

        acc = lax.fori_loop(0, nck, cnt_chunk, zero)
        return jnp.sum(acc.astype(I32).astype(F32), axis=0, keepdims=True)

    def greedy(ref, target, cnt0):
        def step(it, carry):
            t_u, cnt_t = carry
            cand_u = t_u | lax.shift_left(jnp.int32(1), 15 - it)
            cnt = count_ge(ref, cand_u + I16_MIN)
            ok = cnt >= target
            return jnp.where(ok, cand_u, t_u), jnp.where(ok, cnt, cnt_t)
        t_u, cnt_t = lax.fori_loop(0, 16, step, (jnp.zeros((1, tq), I32), cnt0))
        return t_u + I16_MIN, cnt_t

    total = jnp.full((1, tq), 1.0, F32) * (nck * tk).astype(F32)
    h_thr, cnt_ge_h = greedy(hi_ref, n_sel, total)
    cnt_gt_h = jnp.where(h_thr == I16_MAX, 0.0,
                         count_ge(hi_ref, jnp.minimum(h_thr + 1, I16_MAX)))
    h_b = jnp.broadcast_to(h_thr, (PACK, tq)).astype(I16)

    def mask_chunk(c, carry):
        hi = hi_ref[c]
        lo = lo_ref[c]
        for r in range(tk // PACK):
            rows = slice(r * PACK, (r + 1) * PACK)
            lo_ref[c, rows, :] = jnp.where(hi[rows, :] == h_b, lo[rows, :],
                                           jnp.full((PACK, tq), I16_MIN, I16))
        return carry

    lax.fori_loop(0, nck, mask_chunk, 0)
    l_thr, cnt_ge_l = greedy(lo_ref, n_sel - cnt_gt_h, cnt_ge_h - cnt_gt_h)
    n_ge = cnt_gt_h + cnt_ge_l
    bad = jnp.maximum(jnp.abs(n_ge - n_sel), jnp.where(h_thr == I16_MIN, 1.0, 0.0))
    exact = jnp.max(bad) == 0.0

    @pl.when(jnp.logical_not(exact))
    def _():
        cnt_gt_l = jnp.where(l_thr == I16_MAX, 0.0,
                             count_ge(lo_ref, jnp.minimum(l_thr + 1, I16_MAX)))
        need_ref[...] = n_sel - cnt_gt_h - cnt_gt_l
        run_ref[...] = jnp.zeros_like(run_ref)

    q = q_ref[...]
    qz = [q[(hd // 2) * LANES:(hd // 2 + 1) * LANES, :] * half_b[hd % 2]
          for hd in range(N_ATTN_HEADS)]
    take_all = l_thr == I16_MIN
    h_sel = jnp.broadcast_to(jnp.where(take_all, h_thr - 1, h_thr), (PACK, tq)).astype(I16)
    l_sel = jnp.broadcast_to(jnp.where(take_all, I16_MIN + 1, l_thr), (PACK, tq)).astype(I16)
    k_off_v = lax.broadcasted_iota(I32, (tkv, tq), 0)
    q_pos_v = i * tq + lax.broadcasted_iota(I32, (tkv, tq), 1)

    def make_bias(c, with_ties):
        sub = pl.ds(pl.multiple_of((c % (tk // tkv)) * tkv, tkv), tkv)
        hi = hi_ref[c // (tk // tkv), sub, :]
        lo = lo_ref[c // (tk // tkv), sub, :]

        def threshold_mask():
            for r in range(tkv // PACK):
                rows = slice(r * PACK, (r + 1) * PACK)
                sel = (hi[rows, :] > h_sel) | (lo[rows, :] >= l_sel)
                bias_ref[rows, :] = jnp.where(sel, jnp.zeros((PACK, tq), BF16),
                                              jnp.full((PACK, tq), NEG_BIG, BF16)).astype(F32)

        if not with_ties:
            threshold_mask()
            return
        pl.when(exact)(threshold_mask)

        @pl.when(jnp.logical_not(exact))
        def _():
            hi32 = hi.astype(I32)
            lo32 = lo.astype(I32)
            is_h = hi32 == h_thr
            eq = is_h & (lo32 == l_thr)
            gt = (hi32 > h_thr) | (is_h & (lo32 > l_thr))
            tri = (lax.broadcasted_iota(I32, (tkv, tkv), 1)
                   <= lax.broadcasted_iota(I32, (tkv, tkv), 0))
            prefix = jnp.dot(jnp.where(tri, 1.0, 0.0).astype(BF16),
                             jnp.where(eq, 1.0, 0.0).astype(BF16),
                             preferred_element_type=F32) + run_ref[...]
            take = (gt | (eq & (prefix <= need_ref[...]))) & (c * tkv + k_off_v <= q_pos_v)
            bias_ref[...] = jnp.where(take, 0.0, NEG_BIG)
            run_ref[...] += jnp.sum(jnp.where(eq, 1.0, 0.0), axis=0, keepdims=True)

    def attention(online, skewed):
        if online:
            m_ref[...] = jnp.full_like(m_ref, NEG_BIG)
        l_ref[...] = jnp.zeros_like(l_ref)
        acc_ref[...] = jnp.zeros_like(acc_ref)

        def probs(c, slot):
            make_bias(c, with_ties=not skewed)
            start = pl.multiple_of(c * tkv, tkv)
            for hd in range(N_ATTN_HEADS):
                s = jnp.dot(k_ref[hd // 2, pl.ds(start, tkv), :], qz[hd],
                            preferred_element_type=F32) + bias_ref[...]
                if online:
                    m_old = m_ref[hd]
                    m_new = jnp.maximum(m_old, jnp.max(s, axis=0, keepdims=True))
                    alpha = jnp.exp2(m_old - m_new)
                    p = jnp.exp2(s - m_new)
                    l_ref[hd] = alpha * l_ref[hd] + jnp.sum(p, axis=0, keepdims=True)
                    acc_ref[hd] = alpha * acc_ref[hd]
                    m_ref[hd] = m_new
                else:
                    p = jnp.exp2(s)
                    l_ref[hd] += jnp.sum(p, axis=0, keepdims=True)
                p_ref[slot, hd] = p.astype(BF16)

        def weigh_values(c, slot):
            for hd in range(N_ATTN_HEADS):
                acc_ref[hd] += jnp.dot(vt_ref[c, hd * HEAD_DIM:(hd + 1) * HEAD_DIM, :],
                                       p_ref[slot, hd], preferred_element_type=F32)

        n_att = (i * tq) // tkv + 1
        if not skewed:
            def att_chunk(c, carry):
                probs(c, 0)
                weigh_values(c, 0)
                return carry

            lax.fori_loop(0, n_att, att_chunk, 0)
        else:
            def att_pair(j, carry):
                probs(2 * j + 1, 1)
                weigh_values(2 * j, 0)
                probs(2 * j + 2, 0)
                weigh_values(2 * j + 1, 1)
                return carry

            probs(0, 0)
            n_pairs = (n_att - 1) // 2
            lax.fori_loop(0, n_pairs, att_pair, 0)
            last = 2 * n_pairs

            @pl.when(last + 1 < n_att)
            def _():
                probs(last + 1, 1)
                weigh_values(last, 0)
                weigh_values(last + 1, 1)

            @pl.when(last + 1 == n_att)
            def _():
                weigh_values(last, 0)

    bounded = flag_ref[0] != 0

    @pl.when(bounded & exact)
    def _():
        attention(online=False, skewed=True)

    @pl.when(bounded & jnp.logical_not(exact))
    def _():
        attention(online=False, skewed=False)

    @pl.when(jnp.logical_not(bounded))
    def _():
        attention(online=True, skewed=False)

    for pair in range(N_ATTN_HEADS // 2):
        both = jnp.concatenate([acc_ref[2 * pair] / l_ref[2 * pair],
                                acc_ref[2 * pair + 1] / l_ref[2 * pair + 1]], axis=0)
        o_ref[:, pair * LANES:(pair + 1) * LANES] = both.T.astype(BF16)


def _dsa_call(flag, q, k, vt, iq, ik, iwt, *, tq, tk):
    b, _, s, _ = k.shape
    tkv = vt.shape[-1]
    n_sel = min(TOPK_MAX, s // 4)
    tile_t = lambda w: pl.BlockSpec((None, w, tq), lambda bi, i, f: (bi, 0, i))
    whole = lambda *shape: pl.BlockSpec((None,) + shape, lambda bi, i, f: (bi,) + (0,) * len(shape),
                                        pipeline_mode=pl.Buffered(1))
    grid_spec = pltpu.PrefetchScalarGridSpec(
        num_scalar_prefetch=1,
        grid=(b, s // tq),
        in_specs=[tile_t(ATTN_W), tile_t(IDX_W), tile_t(SUBLANES),
                  whole(N_ATTN_HEADS // 2, s, LANES), whole(s // tkv, ATTN_W, tkv),
                  whole(s, LANES)],
        out_specs=pl.BlockSpec((None, tq, ATTN_W), lambda bi, i, f: (bi, i, 0)),
        scratch_shapes=[
            pltpu.VMEM((s // tk, tk, tq), I16),
            pltpu.VMEM((s // tk, tk, tq), I16),
            pltpu.VMEM((tkv, tq), F32),
            pltpu.VMEM((2, N_ATTN_HEADS, tkv, tq), BF16),
            pltpu.VMEM((N_ATTN_HEADS, 1, tq), F32),
            pltpu.VMEM((N_ATTN_HEADS, 1, tq), F32),
            pltpu.VMEM((N_ATTN_HEADS, HEAD_DIM, tq), F32),
            pltpu.VMEM((1, tq), F32),
            pltpu.VMEM((1, tq), F32),
        ])
    return pl.pallas_call(
        functools.partial(_dsa_kernel, seq=s, tq=tq, tk=tk, tkv=tkv, n_sel=float(n_sel)),
        grid_spec=grid_spec,
        out_shape=jax.ShapeDtypeStruct((b, s, ATTN_W), BF16),
        compiler_params=pltpu.CompilerParams(
            dimension_semantics=("arbitrary", "arbitrary"), vmem_limit_bytes=VMEM_LIMIT_BYTES),
        name="dsa",
    )(flag, q, iq, iwt, k, vt, ik)


def _prep_w_in(w_in):
    offs = [sum(IN_SIZES[:n + 1]) for n in range(len(IN_SIZES) - 1)]
    aq, ak, av, iq, ik, iw, pu, mq = jnp.split(w_in, offs, axis=-1)
    pad = jnp.zeros(w_in.shape[:-1] + (LANES - N_IDX_HEADS,), w_in.dtype)
    return jnp.concatenate([aq, ak, av, iq, ik, ik, pu, mq, iw, pad], axis=-1).astype(BF16)


def _prep_ffn(gate, up, down, fc):
    depth, d, f = gate.shape
    split_cols = lambda w: w.astype(BF16).reshape(depth, d, f // fc, fc).transpose(0, 2, 1, 3)
    return split_cols(gate), split_cols(up), down.astype(BF16).reshape(depth, f // fc, fc, d)


def _block_diag(w_pool):
    depth, g, c, _ = w_pool.shape
    eye = jnp.eye(g, dtype=w_pool.dtype)
    return jnp.einsum("lgcd,gh->lgchd", w_pool, eye).reshape(depth, g * c, g * c).astype(BF16)


def kernel(x, mem, ffn1_norm, ffn1_gate, ffn1_up, ffn1_down, mix_norm, mem_norm, w_in, q_norm,
           k_norm, w_pool, pool_scale, w_mem_kv, mq_norm, mk_norm, w_out, ffn2_norm, ffn2_gate,
           ffn2_up, ffn2_down):
    b, s, d = x.shape
    depth = w_in.shape[0]
    tm = min(512, s)
    tq = min(256, s)
    tk = min(512, s)
    tkv = min(512, s)
    assert s % tm == 0 and s % tk == 0 and tk % tq == 0 and tk % tkv == 0 and tm % tkv == 0

    vec = lambda a, reps=1: jnp.tile(a, (1, reps))[:, None, :]
    ffn1 = _prep_ffn(ffn1_gate, ffn1_up, ffn1_down, MXU_N)
    ffn2 = _prep_ffn(ffn2_gate, ffn2_up, ffn2_down, MXU_N)
    w_in_p = _prep_w_in(w_in)
    w_out_b = w_out.astype(BF16)
    w_pool_bd = _block_diag(w_pool)
    head = jnp.arange(ATTN_W) // HEAD_DIM
    bd = jnp.where(head[:, None] == head[None, :], 1.0 / HEAD_DIM, 0.0).astype(BF16)
    qg, kg = vec(q_norm, N_ATTN_HEADS), vec(k_norm, N_ATTN_HEADS)
    mqg, mkg = vec(mq_norm, N_MEM_HEADS), vec(mk_norm, N_MEM_HEADS)
    f1g, f2g, mixg, memg, ps = vec(ffn1_norm), vec(ffn2_norm), vec(mix_norm), vec(mem_norm), vec(pool_scale)
    log2_bound = (NORM_SLACK * LOG2E * jnp.max(jnp.abs(q_norm), axis=-1)
                  * jnp.max(jnp.abs(k_norm), axis=-1))
    bounded = (log2_bound <= SAFE_LOG2_BOUND).astype(I32)

    mk, mv = _memkv_call(mem, memg, w_mem_kv.astype(BF16), mkg, bd)

    x2d = x.reshape(b * s, d)
    for l in range(depth):
        x2d = _ffn_call(x2d, l, f1g, *ffn1, tm=tm)
        q, k, vt, iq, ik, iwt, y_pm = _proj_call(
            x2d.reshape(b, s, d), l, mixg, w_in_p, qg, kg, mqg, bd, w_pool_bd, ps, mk, mv,
            tm=tm, tkv=tkv)
        y_attn = _dsa_call(bounded[l:l + 1], q, k, vt, iq, ik, iwt, tq=tq, tk=tk)
        x2d = _ffn_call(x2d, l, f2g, *ffn2, tm=tm,
                        mix=(y_attn.reshape(b * s, ATTN_W), y_pm.reshape(b * s, POOL_W + MEM_W),
                             w_out_b))
    return x2d.reshape(b, s, d)
```

```python
import functools
import math

import jax
import jax.numpy as jnp
from jax import lax
from jax.experimental import pallas as pl
from jax.experimental.pallas import tpu as pltpu

F32 = jnp.float32
BF16 = jnp.bfloat16
I32 = jnp.int32
I16 = jnp.int16

EPS = 1e-6
HEAD_DIM = 64
N_ATTN_HEADS = 8
N_IDX_HEADS = 4
N_POOL_GROUPS = 4
N_MEM_HEADS = 4
POOL_WINDOWS = (2, 4, 8, 16)
TOPK_MAX = 256

ATTN_W = N_ATTN_HEADS * HEAD_DIM
IDX_W = N_IDX_HEADS * HEAD_DIM
POOL_W = N_POOL_GROUPS * HEAD_DIM
MEM_W = N_MEM_HEADS * HEAD_DIM
IN_SIZES = (ATTN_W, ATTN_W, ATTN_W, IDX_W, HEAD_DIM, N_IDX_HEADS, POOL_W, MEM_W)

LANES = 128
SUBLANES = 8
PACK = 16
CNT_ROWS = 4 * PACK
MXU_N = 256
VMEM_LIMIT_BYTES = 62 * 1024 * 1024
INT_MIN = -(2 ** 31)
I16_MIN = -(2 ** 15)
I16_MAX = 2 ** 15 - 1
NEG_BIG = -1e30
LOG2E = math.log2(math.e)
NORM_SLACK = 8.1
SAFE_LOG2_BOUND = 80.0

OFF_Q, OFF_K, OFF_V = 0, ATTN_W, 2 * ATTN_W
OFF_IQ = 3 * ATTN_W
OFF_IK = OFF_IQ + IDX_W
OFF_PU = OFF_IK + LANES
OFF_MQ = OFF_PU + POOL_W
OFF_IW = OFF_MQ + MEM_W
D_IN_PAD = OFF_IW + LANES

NT_DIMS = (((1,), (1,)), ((), ()))


def _rms(x, g):
    return x * lax.rsqrt(jnp.mean(x * x, axis=-1, keepdims=True) + EPS) * g


def _head_rms(v, gain, bd):
    sq = v * v
    hi = sq.astype(BF16)
    lo = (sq - hi.astype(F32)).astype(BF16)
    ms = (jnp.dot(hi, bd, preferred_element_type=F32)
          + jnp.dot(lo, bd, preferred_element_type=F32))
    return v * lax.rsqrt(ms + EPS) * gain


def _half_masks(rows):
    lane = lax.broadcasted_iota(I32, (rows, LANES), 1)
    lo = lane < HEAD_DIM
    lo_b = jnp.where(lo, 1.0, 0.0).astype(BF16)
    hi_b = jnp.where(lo, 0.0, 1.0).astype(BF16)
    return lo, (lo_b, hi_b)


def _ffn_kernel(*refs, mixed, n_chunks, fc):
    if mixed:
        (x_ref, ya_ref, yb_ref, wo_ref, g_ref, wg_ref, wu_ref, wd_ref,
         o_ref, h_ref, acc_ref) = refs
        x = (x_ref[...]
             + jnp.dot(ya_ref[...], wo_ref[:ATTN_W, :], preferred_element_type=F32)
             + jnp.dot(yb_ref[...], wo_ref[ATTN_W:, :], preferred_element_type=F32))
    else:
        x_ref, g_ref, wg_ref, wu_ref, wd_ref, o_ref, h_ref, acc_ref = refs
        x = x_ref[...]
    o_ref[...] = x
    h_ref[...] = _rms(x, g_ref[...]).astype(BF16)
    acc_ref[...] = jnp.zeros_like(acc_ref)

    for c in range(n_chunks):
        cols = slice(c * fc, (c + 1) * fc)
        h = h_ref[...]
        g = jnp.dot(h, wg_ref[:, cols], preferred_element_type=F32)
        u = jnp.dot(h, wu_ref[:, cols], preferred_element_type=F32)
        a = (g * (1.0 / (1.0 + jnp.exp(-g))) * u).astype(BF16)
        acc_ref[...] += jnp.dot(a, wd_ref[cols, :], preferred_element_type=F32)
    o_ref[...] += 0.5 * acc_ref[...]


def _ffn_call(x2d, layer, gain, wg, wu, wd, mix=None, *, tm, fc):
    n, d = x2d.shape
    f = wg.shape[2]
    assert f % fc == 0
    n_chunks = f // fc
    const = lambda *shape: pl.BlockSpec((None,) + shape, lambda i: (layer,) + (0,) * len(shape),
                                        pipeline_mode=pl.Buffered(1))
    row = lambda w: pl.BlockSpec((tm, w), lambda i: (i, 0))
    in_specs, args = [row(d)], [x2d]
    if mix is not None:
        ya, yb, wo = mix
        in_specs += [row(ya.shape[1]), row(yb.shape[1]), const(*wo.shape[1:])]
        args += [ya, yb, wo]
    in_specs += [const(1, d), const(d, f), const(d, f), const(f, d)]
    args += [gain, wg, wu, wd]
    return pl.pallas_call(
        functools.partial(_ffn_kernel, mixed=mix is not None, n_chunks=n_chunks, fc=fc),
        grid=(n // tm,),
        in_specs=in_specs,
        out_specs=row(d),
        out_shape=jax.ShapeDtypeStruct((n, d), F32),
        scratch_shapes=[pltpu.VMEM((tm, d), BF16), pltpu.VMEM((tm, d), F32)],
        compiler_params=pltpu.CompilerParams(
            dimension_semantics=("arbitrary",), vmem_limit_bytes=VMEM_LIMIT_BYTES),
        name="ffn_mixed" if mix is not None else "ffn",
    )(*args)


def _memkv_kernel(mem_ref, g_ref, w_ref, kg_ref, bd_ref, mk_ref, mv_ref):
    mem_n = _rms(mem_ref[...], g_ref[...]).astype(BF16)
    kv = jnp.dot(mem_n, w_ref[...], preferred_element_type=F32)
    mk_ref[...] = _head_rms(kv[:, :MEM_W], kg_ref[...], bd_ref[:MEM_W, :MEM_W]).astype(BF16)
    mv_ref[...] = kv[:, MEM_W:].astype(BF16)


def _memkv_call(mem, mem_norm, w_kv, mk_gain, bd):
    b, m, d = mem.shape
    depth = w_kv.shape[0]
    out = jax.ShapeDtypeStruct((depth, b, m, MEM_W), BF16)
    out_spec = pl.BlockSpec((None, None, m, MEM_W), lambda l, i: (l, i, 0, 0))
    return pl.pallas_call(
        _memkv_kernel,
        grid=(depth, b),
        in_specs=[
            pl.BlockSpec((None, m, d), lambda l, i: (i, 0, 0)),
            pl.BlockSpec((None, 1, d), lambda l, i: (l, 0, 0)),
            pl.BlockSpec((None, d, 2 * MEM_W), lambda l, i: (l, 0, 0)),
            pl.BlockSpec((None, 1, MEM_W), lambda l, i: (l, 0, 0)),
            pl.BlockSpec(bd.shape, lambda l, i: (0, 0)),
        ],
        out_specs=[out_spec, out_spec],
        out_shape=[out, out],
        compiler_params=pltpu.CompilerParams(dimension_semantics=("arbitrary", "arbitrary")),
        name="memkv",
    )(mem, mem_norm, w_kv, mk_gain, bd)


POOL_HALO = 32


def _proj_kernel(x_ref, g_ref, w_ref, qg_ref, kg_ref, mqg_ref, bd_ref, wp_ref, ps_ref,
                 mk_ref, mv_ref,
                 q_ref, k_ref, vt_ref, iq_ref, ik_ref, iwt_ref, y_ref,
                 p_ref, s2_ref, s4_ref, s8_ref, *, tm, tkv):
    j = pl.program_id(1)
    rows = tm + POOL_HALO
    h = _rms(x_ref[...], g_ref[...]).astype(BF16)
    z = jnp.dot(h, w_ref[...], preferred_element_type=F32)
    bd = bd_ref[...]

    q_ref[...] = (_head_rms(z[:, OFF_Q:OFF_Q + ATTN_W], qg_ref[...], bd)
                  * (HEAD_DIM ** -0.5 * LOG2E)).T.astype(BF16)
    kn = _head_rms(z[:, OFF_K:OFF_K + ATTN_W], kg_ref[...], bd).astype(BF16)
    for pair in range(N_ATTN_HEADS // 2):
        k_ref[pair] = kn[:, pair * LANES:(pair + 1) * LANES]
    vt = z[:, OFF_V:OFF_V + ATTN_W].T
    for part in range(tm // tkv):
        vt_ref[part] = vt[:, part * tkv:(part + 1) * tkv].astype(BF16)
    iq_ref[...] = (z[:, OFF_IQ:OFF_IQ + IDX_W] * HEAD_DIM ** -0.5).T.astype(BF16)
    ik_ref[...] = z[:, OFF_IK:OFF_IK + LANES].astype(BF16)
    iwt_ref[...] = (z[:, OFF_IW:OFF_IW + LANES] * N_IDX_HEADS ** -0.5).T[0:SUBLANES, :]

    @pl.when(j == 0)
    def _():
        p_ref[0:POOL_HALO, :] = jnp.zeros((POOL_HALO, POOL_W), F32)

    pu = z[:, OFF_PU:OFF_PU + POOL_W]
    p_ref[POOL_HALO:rows, :] = pu
    s2_ref[8:rows, :] = p_ref[8:rows, :] + p_ref[7:rows - 1, :]
    s4_ref[16:rows, :] = s2_ref[16:rows, :] + s2_ref[14:rows - 2, :]
    s8_ref[24:rows, :] = s4_ref[24:rows, :] + s4_ref[20:rows - 4, :]
    s16 = s8_ref[POOL_HALO:rows, :] + s8_ref[24:rows - 8, :]
    lane = lax.broadcasted_iota(I32, (tm, POOL_W), 1)
    g1, g2, g3 = lane < HEAD_DIM, lane < 2 * HEAD_DIM, lane < 3 * HEAD_DIM
    wsum = jnp.where(g1, s2_ref[POOL_HALO:rows, :],
                     jnp.where(g2, s4_ref[POOL_HALO:rows, :],
                               jnp.where(g3, s8_ref[POOL_HALO:rows, :], s16)))
    win = jnp.where(g1, float(POOL_WINDOWS[0]),
                    jnp.where(g2, float(POOL_WINDOWS[1]),
                              jnp.where(g3, float(POOL_WINDOWS[2]), float(POOL_WINDOWS[3]))))
    t1 = (j * tm + 1 + lax.broadcasted_iota(I32, (tm, POOL_W), 0)).astype(F32)
    pooled = (wsum / jnp.minimum(t1, win) - pu).astype(BF16)
    y_pool = jnp.dot(pooled, wp_ref[...], preferred_element_type=F32) * ps_ref[...]
    y_ref[:, 0:POOL_W] = y_pool.astype(BF16)
    p_ref[16:POOL_HALO, :] = p_ref[tm + 16:rows, :]

    lo, half_b = _half_masks(tm)
    mqn = (_head_rms(z[:, OFF_MQ:OFF_MQ + MEM_W], mqg_ref[...], bd[:MEM_W, :MEM_W])
           * HEAD_DIM ** -0.5).astype(BF16)
    mk = mk_ref[...]
    mv = mv_ref[...]
    for pair in range(N_MEM_HEADS // 2):
        cols = slice(pair * LANES, (pair + 1) * LANES)
        outs = []
        for sub in range(2):
            logits = lax.dot_general(mqn[:, cols] * half_b[sub], mk[:, cols], NT_DIMS,
                                     preferred_element_type=F32)
            e = jnp.exp(logits - jnp.max(logits, axis=-1, keepdims=True))
            p = (e / jnp.sum(e, axis=-1, keepdims=True)).astype(BF16)
            outs.append(jnp.dot(p, mv[:, cols], preferred_element_type=F32))
        y_ref[:, POOL_W + pair * LANES:POOL_W + (pair + 1) * LANES] = (
            jnp.where(lo, outs[0], outs[1]).astype(BF16))


def _proj_call(x, layer, gain, w_in, qg, kg, mqg, bd, wp, ps, mk, mv, *, tm, tkv):
    b, s, d = x.shape
    m = mk.shape[2]
    lay = lambda *shape: pl.BlockSpec((None,) + shape, lambda i, j: (layer,) + (0,) * len(shape),
                                      pipeline_mode=pl.Buffered(1))
    tile = lambda w: pl.BlockSpec((None, tm, w), lambda i, j: (i, j, 0))
    mem = pl.BlockSpec((None, None, m, MEM_W), lambda i, j: (layer, i, 0, 0))
    sds = jax.ShapeDtypeStruct
    tile_t = lambda w: pl.BlockSpec((None, w, tm), lambda i, j: (i, 0, j))
    out_shape = [sds((b, ATTN_W, s), BF16),
                 sds((b, N_ATTN_HEADS // 2, s, LANES), BF16),
                 sds((b, s // tkv, ATTN_W, tkv), BF16),
                 sds((b, IDX_W, s), BF16),
                 sds((b, s, LANES), BF16),
                 sds((b, SUBLANES, s), F32),
                 sds((b, s, POOL_W + MEM_W), BF16)]
    out_specs = [tile_t(ATTN_W),
                 pl.BlockSpec((None, N_ATTN_HEADS // 2, tm, LANES), lambda i, j: (i, 0, j, 0)),
                 pl.BlockSpec((None, tm // tkv, ATTN_W, tkv), lambda i, j: (i, j, 0, 0)),
                 tile_t(IDX_W), tile(LANES), tile_t(SUBLANES),
                 tile(POOL_W + MEM_W)]
    rows = tm + POOL_HALO
    return pl.pallas_call(
        functools.partial(_proj_kernel, tm=tm, tkv=tkv),
        grid=(b, s // tm),
        in_specs=[tile(d), lay(1, d), lay(d, D_IN_PAD), lay(1, ATTN_W), lay(1, ATTN_W),
                  lay(1, MEM_W), pl.BlockSpec(bd.shape, lambda i, j: (0, 0)),
                  lay(POOL_W, POOL_W), lay(1, POOL_W), mem, mem],
        out_specs=out_specs,
        out_shape=out_shape,
        scratch_shapes=[pltpu.VMEM((rows, POOL_W), F32) for _ in range(4)],
        compiler_params=pltpu.CompilerParams(
            dimension_semantics=("arbitrary", "arbitrary"), vmem_limit_bytes=VMEM_LIMIT_BYTES),
        name="proj",
    )(x, gain, w_in, qg, kg, mqg, bd, wp, ps, mk, mv)


def _dsa_kernel(flag_ref, q_ref, iq_ref, iw_ref, k_ref, vt_ref, ik_ref, o_ref,
                hi_ref, lo_ref, bias_ref, p_ref, m_ref, l_ref, acc_ref, run_ref, need_ref,
                *, seq, tq, tk, tkv, n_sel):
    i = pl.program_id(1)
    nck = (i * tq) // tk + 1
    top = lax.broadcasted_iota(I32, (LANES, tq), 0) < HEAD_DIM
    half_b = (jnp.where(top, 1.0, 0.0).astype(BF16), jnp.where(top, 0.0, 1.0).astype(BF16))
    q_pos = i * tq + lax.broadcasted_iota(I32, (tk, tq), 1)
    k_off = lax.broadcasted_iota(I32, (tk, tq), 0)

    iq = iq_ref[...]
    iw = iw_ref[...]
    iqz = [iq[(hd // 2) * LANES:(hd // 2 + 1) * LANES, :] * half_b[hd % 2]
           for hd in range(N_IDX_HEADS)]
    wrow = [iw[hd:hd + 1, :] for hd in range(N_IDX_HEADS)]

    def score_chunk(c, causal):
        ikc = ik_ref[pl.ds(pl.multiple_of(c * tk, tk), tk), :]
        score = None
        for hd in range(N_IDX_HEADS):
            d = jnp.dot(ikc, iqz[hd], preferred_element_type=F32)
            t = jnp.maximum(d, 0.0) * wrow[hd]
            score = t if score is None else score + t
        bits = pltpu.bitcast(score, I32)
        neg = bits >> 31
        key = (bits ^ (neg & 0x7FFFFFFF)) - neg
        key = jnp.where(key == 0, (seq - 1 - c * tk) - k_off, key + (~neg & seq))
        if causal:
            key = jnp.where(c * tk + k_off <= q_pos, key, INT_MIN)
        hi_ref[c] = (key >> 16).astype(I16)
        lo_ref[c] = (key ^ 0x8000).astype(I16)
        return 0

    def score_pair(j, carry):
        score_chunk(2 * j, False)
        score_chunk(2 * j + 1, False)
        return carry

    lax.fori_loop(0, (nck - 1) // 2, score_pair, 0)

    @pl.when((nck - 1) % 2 == 1)
    def _():
        score_chunk(nck - 2, False)

    score_chunk(nck - 1, True)

    def count_ge(ref, cand):
        cand_b = jnp.broadcast_to(cand, (CNT_ROWS, tq)).astype(I16)
        one = jnp.ones((CNT_ROWS, tq), I16)
        zero = jnp.zeros((CNT_ROWS, tq), I16)

        def cnt_chunk(c, acc):
            a = ref[c]
            for r in range(tk // CNT_ROWS):
                acc = acc + jnp.where(a[r * CNT_ROWS:(r + 1) * CNT_ROWS, :] >= cand_b, one, zero)
            return acc

        acc = lax.fori_loop(0, nck, cnt_chunk, zero)
        return jnp.sum(acc.astype(I32).astype(F32), axis=0, keepdims=True)

    def greedy(ref, target, cnt0):
        def step(it, carry):
            t_u, cnt_t = carry
            cand_u = t_u | lax.shift_left(jnp.int32(1), 15 - it)
            cnt = count_ge(ref, cand_u + I16_MIN)
            ok = cnt >= target
            return jnp.where(ok, cand_u, t_u), jnp.where(ok, cnt, cnt_t)
        t_u, cnt_t = lax.fori_loop(0, 16, step, (jnp.zeros((1, tq), I32), cnt0))
        return t_u + I16_MIN, cnt_t

    total = jnp.full((1, tq), 1.0, F32) * (nck * tk).astype(F32)
    h_thr, cnt_ge_h = greedy(hi_ref, n_sel, total)
    cnt_gt_h = jnp.where(h_thr == I16_MAX, 0.0,
                         count_ge(hi_ref, jnp.minimum(h_thr + 1, I16_MAX)))
    h_b = jnp.broadcast_to(h_thr, (PACK, tq)).astype(I16)

    def mask_chunk(c, carry):
        hi = hi_ref[c]
        lo = lo_ref[c]
        for r in range(tk // PACK):
            rows = slice(r * PACK, (r + 1) * PACK)
            lo_ref[c, rows, :] = jnp.where(hi[rows, :] == h_b, lo[rows, :],
                                           jnp.full((PACK, tq), I16_MIN, I16))
        return carry

    lax.fori_loop(0, nck, mask_chunk, 0)
    l_thr, cnt_ge_l = greedy(lo_ref, n_sel - cnt_gt_h, cnt_ge_h - cnt_gt_h)
    n_ge = cnt_gt_h + cnt_ge_l
    bad = jnp.maximum(jnp.abs(n_ge - n_sel), jnp.where(h_thr == I16_MIN, 1.0, 0.0))
    exact = jnp.max(bad) == 0.0

    @pl.when(jnp.logical_not(exact))
    def _():
        cnt_gt_l = jnp.where(l_thr == I16_MAX, 0.0,
                             count_ge(lo_ref, jnp.minimum(l_thr + 1, I16_MAX)))
        need_ref[...] = n_sel - cnt_gt_h - cnt_gt_l
        run_ref[...] = jnp.zeros_like(run_ref)

    q = q_ref[...]
    qz = [q[(hd // 2) * LANES:(hd // 2 + 1) * LANES, :] * half_b[hd % 2]
          for hd in range(N_ATTN_HEADS)]
    take_all = l_thr == I16_MIN
    h_sel = jnp.broadcast_to(jnp.where(take_all, h_thr - 1, h_thr), (PACK, tq)).astype(I16)
    l_sel = jnp.broadcast_to(jnp.where(take_all, I16_MIN + 1, l_thr), (PACK, tq)).astype(I16)
    k_off_v = lax.broadcasted_iota(I32, (tkv, tq), 0)
    q_pos_v = i * tq + lax.broadcasted_iota(I32, (tkv, tq), 1)

    def make_bias(c, with_ties):
        sub = pl.ds(pl.multiple_of((c % (tk // tkv)) * tkv, tkv), tkv)
        hi = hi_ref[c // (tk // tkv), sub, :]
        lo = lo_ref[c // (tk // tkv), sub, :]

        def threshold_mask():
            for r in range(tkv // PACK):
                rows = slice(r * PACK, (r + 1) * PACK)
                sel = (hi[rows, :] > h_sel) | (lo[rows, :] >= l_sel)
                bias_ref[rows, :] = jnp.where(sel, jnp.zeros((PACK, tq), BF16),
                                              jnp.full((PACK, tq), NEG_BIG, BF16)).astype(F32)

        if not with_ties:
            threshold_mask()
            return
        pl.when(exact)(threshold_mask)

        @pl.when(jnp.logical_not(exact))
        def _():
            hi32 = hi.astype(I32)
            lo32 = lo.astype(I32)
            is_h = hi32 == h_thr
            eq = is_h & (lo32 == l_thr)
            gt = (hi32 > h_thr) | (is_h & (lo32 > l_thr))
            tri = (lax.broadcasted_iota(I32, (tkv, tkv), 1)
                   <= lax.broadcasted_iota(I32, (tkv, tkv), 0))
            prefix = jnp.dot(jnp.where(tri, 1.0, 0.0).astype(BF16),
                             jnp.where(eq, 1.0, 0.0).astype(BF16),
                             preferred_element_type=F32) + run_ref[...]
            take = (gt | (eq & (prefix <= need_ref[...]))) & (c * tkv + k_off_v <= q_pos_v)
            bias_ref[...] = jnp.where(take, 0.0, NEG_BIG)
            run_ref[...] += jnp.sum(jnp.where(eq, 1.0, 0.0), axis=0, keepdims=True)

    def attention(online, skewed):
        if online:
            m_ref[...] = jnp.full_like(m_ref, NEG_BIG)
        l_ref[...] = jnp.zeros_like(l_ref)
        acc_ref[...] = jnp.zeros_like(acc_ref)

        def probs(c, slot):
            make_bias(c, with_ties=not skewed)
            start = pl.multiple_of(c * tkv, tkv)
            for hd in range(N_ATTN_HEADS):
                s = jnp.dot(k_ref[hd // 2, pl.ds(start, tkv), :], qz[hd],
                            preferred_element_type=F32) + bias_ref[...]
                if online:
                    m_old = m_ref[hd]
                    m_new = jnp.maximum(m_old, jnp.max(s, axis=0, keepdims=True))
                    alpha = jnp.exp2(m_old - m_new)
                    p = jnp.exp2(s - m_new)
                    l_ref[hd] = alpha * l_ref[hd] + jnp.sum(p, axis=0, keepdims=True)
                    acc_ref[hd] = alpha * acc_ref[hd]
                    m_ref[hd] = m_new
                else:
                    p = jnp.exp2(s)
                    l_ref[hd] += jnp.sum(p, axis=0, keepdims=True)
                p_ref[slot, hd] = p.astype(BF16)

        def weigh_values(c, slot):
            for hd in range(N_ATTN_HEADS):
                acc_ref[hd] += jnp.dot(vt_ref[c, hd * HEAD_DIM:(hd + 1) * HEAD_DIM, :],
                                       p_ref[slot, hd], preferred_element_type=F32)

        n_att = (i * tq) // tkv + 1
        if not skewed:
            def att_chunk(c, carry):
                probs(c, 0)
                weigh_values(c, 0)
                return carry

            lax.fori_loop(0, n_att, att_chunk, 0)
        else:
            def att_pair(j, carry):
                probs(2 * j + 1, 1)
                weigh_values(2 * j, 0)
                probs(2 * j + 2, 0)
                weigh_values(2 * j + 1, 1)
                return carry

            probs(0, 0)
            n_pairs = (n_att - 1) // 2
            lax.fori_loop(0, n_pairs, att_pair, 0)
            last = 2 * n_pairs

            @pl.when(last + 1 < n_att)
            def _():
                probs(last + 1, 1)
                weigh_values(last, 0)
                weigh_values(last + 1, 1)

            @pl.when(last + 1 == n_att)
            def _():
                weigh_values(last, 0)

    bounded = flag_ref[0] != 0

    @pl.when(bounded & exact)
    def _():
        attention(online=False, skewed=True)

    @pl.when(bounded & jnp.logical_not(exact))
    def _():
        attention(online=False, skewed=False)

    @pl.when(jnp.logical_not(bounded))
    def _():
        attention(online=True, skewed=False)

    for pair in range(N_ATTN_HEADS // 2):
        both = jnp.concatenate([acc_ref[2 * pair] / l_ref[2 * pair],
                                acc_ref[2 * pair + 1] / l_ref[2 * pair + 1]], axis=0)
        o_ref[:, pair * LANES:(pair + 1) * LANES] = both.T.astype(BF16)


def _dsa_call(flag, q, k, vt, iq, ik, iwt, *, tq, tk):
    b, _, s, _ = k.shape
    tkv = vt.shape[-1]
    n_sel = min(TOPK_MAX, s // 4)
    tile_t = lambda w: pl.BlockSpec((None, w, tq), lambda bi, i, f: (bi, 0, i))
    whole = lambda *shape: pl.BlockSpec((None,) + shape, lambda bi, i, f: (bi,) + (0,) * len(shape),
                                        pipeline_mode=pl.Buffered(1))
    grid_spec = pltpu.PrefetchScalarGridSpec(
        num_scalar_prefetch=1,
        grid=(b, s // tq),
        in_specs=[tile_t(ATTN_W), tile_t(IDX_W), tile_t(SUBLANES),
                  whole(N_ATTN_HEADS // 2, s, LANES), whole(s // tkv, ATTN_W, tkv),
                  whole(s, LANES)],
        out_specs=pl.BlockSpec((None, tq, ATTN_W), lambda bi, i, f: (bi, i, 0)),
        scratch_shapes=[
            pltpu.VMEM((s // tk, tk, tq), I16),
            pltpu.VMEM((s // tk, tk, tq), I16),
            pltpu.VMEM((tkv, tq), F32),
            pltpu.VMEM((2, N_ATTN_HEADS, tkv, tq), BF16),
            pltpu.VMEM((N_ATTN_HEADS, 1, tq), F32),
            pltpu.VMEM((N_ATTN_HEADS, 1, tq), F32),
            pltpu.VMEM((N_ATTN_HEADS, HEAD_DIM, tq), F32),
            pltpu.VMEM((1, tq), F32),
            pltpu.VMEM((1, tq), F32),
        ])
    return pl.pallas_call(
        functools.partial(_dsa_kernel, seq=s, tq=tq, tk=tk, tkv=tkv, n_sel=float(n_sel)),
        grid_spec=grid_spec,
        out_shape=jax.ShapeDtypeStruct((b, s, ATTN_W), BF16),
        compiler_params=pltpu.CompilerParams(
            dimension_semantics=("arbitrary", "arbitrary"), vmem_limit_bytes=VMEM_LIMIT_BYTES),
        name="dsa",
    )(flag, q, iq, iwt, k, vt, ik)


def _prep_w_in(w_in):
    offs = [sum(IN_SIZES[:n + 1]) for n in range(len(IN_SIZES) - 1)]
    aq, ak, av, iq, ik, iw, pu, mq = jnp.split(w_in, offs, axis=-1)
    pad = jnp.zeros(w_in.shape[:-1] + (LANES - N_IDX_HEADS,), w_in.dtype)
    return jnp.concatenate([aq, ak, av, iq, ik, ik, pu, mq, iw, pad], axis=-1).astype(BF16)


def _block_diag(w_pool):
    depth, g, c, _ = w_pool.shape
    eye = jnp.eye(g, dtype=w_pool.dtype)
    return jnp.einsum("lgcd,gh->lgchd", w_pool, eye).reshape(depth, g * c, g * c).astype(BF16)


def kernel(x, mem, ffn1_norm, ffn1_gate, ffn1_up, ffn1_down, mix_norm, mem_norm, w_in, q_norm,
           k_norm, w_pool, pool_scale, w_mem_kv, mq_norm, mk_norm, w_out, ffn2_norm, ffn2_gate,
           ffn2_up, ffn2_down):
    b, s, d = x.shape
    depth = w_in.shape[0]
    tm = min(512, s)
    tq = min(256, s)
    tk = min(512, s)
    tkv = min(512, s)
    assert s % tm == 0 and s % tk == 0 and tk % tq == 0 and tk % tkv == 0 and tm % tkv == 0

    vec = lambda a, reps=1: jnp.tile(a, (1, reps))[:, None, :]
    ffn1 = tuple(w.astype(BF16) for w in (ffn1_gate, ffn1_up, ffn1_down))
    ffn2 = tuple(w.astype(BF16) for w in (ffn2_gate, ffn2_up, ffn2_down))
    w_in_p = _prep_w_in(w_in)
    w_out_b = w_out.astype(BF16)
    w_pool_bd = _block_diag(w_pool)
    head = jnp.arange(ATTN_W) // HEAD_DIM
    bd = jnp.where(head[:, None] == head[None, :], 1.0 / HEAD_DIM, 0.0).astype(BF16)
    qg, kg = vec(q_norm, N_ATTN_HEADS), vec(k_norm, N_ATTN_HEADS)
    mqg, mkg = vec(mq_norm, N_MEM_HEADS), vec(mk_norm, N_MEM_HEADS)
    f1g, f2g, mixg, memg, ps = vec(ffn1_norm), vec(ffn2_norm), vec(mix_norm), vec(mem_norm), vec(pool_scale)
    log2_bound = (NORM_SLACK * LOG2E * jnp.max(jnp.abs(q_norm), axis=-1)
                  * jnp.max(jnp.abs(k_norm), axis=-1))
    bounded = (log2_bound <= SAFE_LOG2_BOUND).astype(I32)

    mk, mv = _memkv_call(mem, memg, w_mem_kv.astype(BF16), mkg, bd)

    x2d = x.reshape(b * s, d)
    for l in range(depth):
        x2d = _ffn_call(x2d, l, f1g, *ffn1, tm=tm, fc=MXU_N)
        q, k, vt, iq, ik, iwt, y_pm = _proj_call(
            x2d.reshape(b, s, d), l, mixg, w_in_p, qg, kg, mqg, bd, w_pool_bd, ps, mk, mv,
            tm=tm, tkv=tkv)
        y_attn = _dsa_call(bounded[l:l + 1], q, k, vt, iq, ik, iwt, tq=tq, tk=tk)
        x2d = _ffn_call(x2d, l, f2g, *ffn2, tm=tm, fc=MXU_N,
                        mix=(y_attn.reshape(b * s, ATTN_W), y_pm.reshape(b * s, POOL_W + MEM_W),
                             w_out_b))
    return x2d.reshape(b, s, d)
```

```python
import functools
import math

import jax
import jax.numpy as jnp
from jax import lax
from jax.experimental import pallas as pl
from jax.experimental.pallas import tpu as pltpu

F32 = jnp.float32
BF16 = jnp.bfloat16
I32 = jnp.int32
I16 = jnp.int16

EPS = 1e-6
HEAD_DIM = 64
N_ATTN_HEADS = 8
N_IDX_HEADS = 4
N_POOL_GROUPS = 4
N_MEM_HEADS = 4
POOL_WINDOWS = (2, 4, 8, 16)
TOPK_MAX = 256

ATTN_W = N_ATTN_HEADS * HEAD_DIM
IDX_W = N_IDX_HEADS * HEAD_DIM
POOL_W = N_POOL_GROUPS * HEAD_DIM
MEM_W = N_MEM_HEADS * HEAD_DIM
IN_SIZES = (ATTN_W, ATTN_W, ATTN_W, IDX_W, HEAD_DIM, N_IDX_HEADS, POOL_W, MEM_W)

LANES = 128
SUBLANES = 8
PACK = 16
CNT_ROWS = 4 * PACK
MXU_N = 256
VMEM_LIMIT_BYTES = 62 * 1024 * 1024
INT_MIN = -(2 ** 31)
I16_MIN = -(2 ** 15)
I16_MAX = 2 ** 15 - 1
NEG_BIG = -1e30
LOG2E = math.log2(math.e)
NORM_SLACK = 8.1
SAFE_LOG2_BOUND = 80.0

OFF_Q, OFF_K, OFF_V = 0, ATTN_W, 2 * ATTN_W
OFF_IQ = 3 * ATTN_W
OFF_IK = OFF_IQ + IDX_W
OFF_PU = OFF_IK + LANES
OFF_MQ = OFF_PU + POOL_W
OFF_IW = OFF_MQ + MEM_W
D_IN_PAD = OFF_IW + LANES

NT_DIMS = (((1,), (1,)), ((), ()))


def _rms(x, g):
    return x * lax.rsqrt(jnp.mean(x * x, axis=-1, keepdims=True) + EPS) * g


def _head_rms(v, gain, bd):
    sq = v * v
    hi = sq.astype(BF16)
    lo = (sq - hi.astype(F32)).astype(BF16)
    ms = jnp.concatenate(
        [jnp.dot(hi[:, c:c + MXU_N], bd, preferred_element_type=F32)
         + jnp.dot(lo[:, c:c + MXU_N], bd, preferred_element_type=F32)
         for c in range(0, v.shape[1], MXU_N)], axis=-1)
    return v * lax.rsqrt(ms + EPS) * gain


def _half_masks(rows):
    lane = lax.broadcasted_iota(I32, (rows, LANES), 1)
    lo = lane < HEAD_DIM
    lo_b = jnp.where(lo, 1.0, 0.0).astype(BF16)
    hi_b = jnp.where(lo, 0.0, 1.0).astype(BF16)
    return lo, (lo_b, hi_b)


def _ffn_kernel(*refs, mixed, n_chunks, fc):
    if mixed:
        (x_ref, ya_ref, yb_ref, wo_ref, g_ref, wg_ref, wu_ref, wd_ref,
         o_ref, h_ref, acc_ref) = refs
        x = (x_ref[...]
             + jnp.dot(ya_ref[...], wo_ref[:ATTN_W, :], preferred_element_type=F32)
             + jnp.dot(yb_ref[...], wo_ref[ATTN_W:, :], preferred_element_type=F32))
    else:
        x_ref, g_ref, wg_ref, wu_ref, wd_ref, o_ref, h_ref, acc_ref = refs
        x = x_ref[...]
    o_ref[...] = x
    h_ref[...] = _rms(x, g_ref[...]).astype(BF16)
    acc_ref[...] = jnp.zeros_like(acc_ref)

    for c in range(n_chunks):
        cols = slice(c * fc, (c + 1) * fc)
        h = h_ref[...]
        g = jnp.dot(h, wg_ref[:, cols], preferred_element_type=F32)
        u = jnp.dot(h, wu_ref[:, cols], preferred_element_type=F32)
        a = (g * (1.0 / (1.0 + jnp.exp(-g))) * u).astype(BF16)
        acc_ref[...] += jnp.dot(a, wd_ref[cols, :], preferred_element_type=F32)
    o_ref[...] += 0.5 * acc_ref[...]


def _ffn_call(x2d, layer, gain, wg, wu, wd, mix=None, *, tm, fc):
    n, d = x2d.shape
    f = wg.shape[2]
    assert f % fc == 0
    n_chunks = f // fc
    const = lambda *shape: pl.BlockSpec((None,) + shape, lambda i: (layer,) + (0,) * len(shape),
                                        pipeline_mode=pl.Buffered(1))
    row = lambda w: pl.BlockSpec((tm, w), lambda i: (i, 0))
    in_specs, args = [row(d)], [x2d]
    if mix is not None:
        ya, yb, wo = mix
        in_specs += [row(ya.shape[1]), row(yb.shape[1]), const(*wo.shape[1:])]
        args += [ya, yb, wo]
    in_specs += [const(1, d), const(d, f), const(d, f), const(f, d)]
    args += [gain, wg, wu, wd]
    return pl.pallas_call(
        functools.partial(_ffn_kernel, mixed=mix is not None, n_chunks=n_chunks, fc=fc),
        grid=(n // tm,),
        in_specs=in_specs,
        out_specs=row(d),
        out_shape=jax.ShapeDtypeStruct((n, d), F32),
        scratch_shapes=[pltpu.VMEM((tm, d), BF16), pltpu.VMEM((tm, d), F32)],
        compiler_params=pltpu.CompilerParams(
            dimension_semantics=("arbitrary",), vmem_limit_bytes=VMEM_LIMIT_BYTES),
        name="ffn_mixed" if mix is not None else "ffn",
    )(*args)


def _memkv_kernel(mem_ref, g_ref, w_ref, kg_ref, bd_ref, mk_ref, mv_ref):
    mem_n = _rms(mem_ref[...], g_ref[...]).astype(BF16)
    kv = jnp.dot(mem_n, w_ref[...], preferred_element_type=F32)
    mk_ref[...] = _head_rms(kv[:, :MEM_W], kg_ref[...], bd_ref[...]).astype(BF16)
    mv_ref[...] = kv[:, MEM_W:].astype(BF16)


def _memkv_call(mem, mem_norm, w_kv, mk_gain, bd):
    b, m, d = mem.shape
    depth = w_kv.shape[0]
    out = jax.ShapeDtypeStruct((depth, b, m, MEM_W), BF16)
    out_spec = pl.BlockSpec((None, None, m, MEM_W), lambda l, i: (l, i, 0, 0))
    return pl.pallas_call(
        _memkv_kernel,
        grid=(depth, b),
        in_specs=[
            pl.BlockSpec((None, m, d), lambda l, i: (i, 0, 0)),
            pl.BlockSpec((None, 1, d), lambda l, i: (l, 0, 0)),
            pl.BlockSpec((None, d, 2 * MEM_W), lambda l, i: (l, 0, 0)),
            pl.BlockSpec((None, 1, MEM_W), lambda l, i: (l, 0, 0)),
            pl.BlockSpec(bd.shape, lambda l, i: (0, 0)),
        ],
        out_specs=[out_spec, out_spec],
        out_shape=[out, out],
        compiler_params=pltpu.CompilerParams(dimension_semantics=("arbitrary", "arbitrary")),
        name="memkv",
    )(mem, mem_norm, w_kv, mk_gain, bd)


POOL_HALO = 32


def _proj_kernel(x_ref, g_ref, w_ref, qg_ref, kg_ref, mqg_ref, bd_ref, wp_ref, ps_ref,
                 mk_ref, mv_ref,
                 q_ref, k_ref, vt_ref, iq_ref, ik_ref, iwt_ref, y_ref,
                 p_ref, s2_ref, s4_ref, s8_ref, *, tm, tkv):
    j = pl.program_id(1)
    rows = tm + POOL_HALO
    h = _rms(x_ref[...], g_ref[...]).astype(BF16)
    z = jnp.dot(h, w_ref[...], preferred_element_type=F32)
    bd = bd_ref[...]

    q_ref[...] = (_head_rms(z[:, OFF_Q:OFF_Q + ATTN_W], qg_ref[...], bd)
                  * (HEAD_DIM ** -0.5 * LOG2E)).T.astype(BF16)
    kn = _head_rms(z[:, OFF_K:OFF_K + ATTN_W], kg_ref[...], bd).astype(BF16)
    for pair in range(N_ATTN_HEADS // 2):
        k_ref[pair] = kn[:, pair * LANES:(pair + 1) * LANES]
    vt = z[:, OFF_V:OFF_V + ATTN_W].T
    for part in range(tm // tkv):
        vt_ref[part] = vt[:, part * tkv:(part + 1) * tkv].astype(BF16)
    iq_ref[...] = (z[:, OFF_IQ:OFF_IQ + IDX_W] * HEAD_DIM ** -0.5).T.astype(BF16)
    ik_ref[...] = z[:, OFF_IK:OFF_IK + LANES].astype(BF16)
    iwt_ref[...] = (z[:, OFF_IW:OFF_IW + LANES] * N_IDX_HEADS ** -0.5).T[0:SUBLANES, :]

    @pl.when(j == 0)
    def _():
        p_ref[0:POOL_HALO, :] = jnp.zeros((POOL_HALO, POOL_W), F32)

    pu = z[:, OFF_PU:OFF_PU + POOL_W]
    p_ref[POOL_HALO:rows, :] = pu
    s2_ref[8:rows, :] = p_ref[8:rows, :] + p_ref[7:rows - 1, :]
    s4_ref[16:rows, :] = s2_ref[16:rows, :] + s2_ref[14:rows - 2, :]
    s8_ref[24:rows, :] = s4_ref[24:rows, :] + s4_ref[20:rows - 4, :]
    s16 = s8_ref[POOL_HALO:rows, :] + s8_ref[24:rows - 8, :]
    lane = lax.broadcasted_iota(I32, (tm, POOL_W), 1)
    g1, g2, g3 = lane < HEAD_DIM, lane < 2 * HEAD_DIM, lane < 3 * HEAD_DIM
    wsum = jnp.where(g1, s2_ref[POOL_HALO:rows, :],
                     jnp.where(g2, s4_ref[POOL_HALO:rows, :],
                               jnp.where(g3, s8_ref[POOL_HALO:rows, :], s16)))
    win = jnp.where(g1, float(POOL_WINDOWS[0]),
                    jnp.where(g2, float(POOL_WINDOWS[1]),
                              jnp.where(g3, float(POOL_WINDOWS[2]), float(POOL_WINDOWS[3]))))
    t1 = (j * tm + 1 + lax.broadcasted_iota(I32, (tm, POOL_W), 0)).astype(F32)
    pooled = (wsum / jnp.minimum(t1, win) - pu).astype(BF16)
    y_pool = jnp.dot(pooled, wp_ref[...], preferred_element_type=F32) * ps_ref[...]
    y_ref[:, 0:POOL_W] = y_pool.astype(BF16)
    p_ref[16:POOL_HALO, :] = p_ref[tm + 16:rows, :]

    lo, half_b = _half_masks(tm)
    mqn = (_head_rms(z[:, OFF_MQ:OFF_MQ + MEM_W], mqg_ref[...], bd)
           * HEAD_DIM ** -0.5).astype(BF16)
    mk = mk_ref[...]
    mv = mv_ref[...]
    for pair in range(N_MEM_HEADS // 2):
        cols = slice(pair * LANES, (pair + 1) * LANES)
        outs = []
        for sub in range(2):
            logits = lax.dot_general(mqn[:, cols] * half_b[sub], mk[:, cols], NT_DIMS,
                                     preferred_element_type=F32)
            e = jnp.exp(logits - jnp.max(logits, axis=-1, keepdims=True))
            p = (e / jnp.sum(e, axis=-1, keepdims=True)).astype(BF16)
            outs.append(jnp.dot(p, mv[:, cols], preferred_element_type=F32))
        y_ref[:, POOL_W + pair * LANES:POOL_W + (pair + 1) * LANES] = (
            jnp.where(lo, outs[0], outs[1]).astype(BF16))


def _proj_call(x, layer, gain, w_in, qg, kg, mqg, bd, wp, ps, mk, mv, *, tm, tkv):
    b, s, d = x.shape
    m = mk.shape[2]
    lay = lambda *shape: pl.BlockSpec((None,) + shape, lambda i, j: (layer,) + (0,) * len(shape),
                                      pipeline_mode=pl.Buffered(1))
    tile = lambda w: pl.BlockSpec((None, tm, w), lambda i, j: (i, j, 0))
    mem = pl.BlockSpec((None, None, m, MEM_W), lambda i, j: (layer, i, 0, 0))
    sds = jax.ShapeDtypeStruct
    tile_t = lambda w: pl.BlockSpec((None, w, tm), lambda i, j: (i, 0, j))
    out_shape = [sds((b, ATTN_W, s), BF16),
                 sds((b, N_ATTN_HEADS // 2, s, LANES), BF16),
                 sds((b, s // tkv, ATTN_W, tkv), BF16),
                 sds((b, IDX_W, s), BF16),
                 sds((b, s, LANES), BF16),
                 sds((b, SUBLANES, s), F32),
                 sds((b, s, POOL_W + MEM_W), BF16)]
    out_specs = [tile_t(ATTN_W),
                 pl.BlockSpec((None, N_ATTN_HEADS // 2, tm, LANES), lambda i, j: (i, 0, j, 0)),
                 pl.BlockSpec((None, tm // tkv, ATTN_W, tkv), lambda i, j: (i, j, 0, 0)),
                 tile_t(IDX_W), tile(LANES), tile_t(SUBLANES),
                 tile(POOL_W + MEM_W)]
    rows = tm + POOL_HALO
    return pl.pallas_call(
        functools.partial(_proj_kernel, tm=tm, tkv=tkv),
        grid=(b, s // tm),
        in_specs=[tile(d), lay(1, d), lay(d, D_IN_PAD), lay(1, ATTN_W), lay(1, ATTN_W),
                  lay(1, MEM_W), pl.BlockSpec(bd.shape, lambda i, j: (0, 0)),
                  lay(POOL_W, POOL_W), lay(1, POOL_W), mem, mem],
        out_specs=out_specs,
        out_shape=out_shape,
        scratch_shapes=[pltpu.VMEM((rows, POOL_W), F32) for _ in range(4)],
        compiler_params=pltpu.CompilerParams(
            dimension_semantics=("arbitrary", "arbitrary"), vmem_limit_bytes=VMEM_LIMIT_BYTES),
        name="proj",
    )(x, gain, w_in, qg, kg, mqg, bd, wp, ps, mk, mv)


def _dsa_kernel(flag_ref, q_ref, iq_ref, iw_ref, k_ref, vt_ref, ik_ref, o_ref,
                hi_ref, lo_ref, bias_ref, p_ref, m_ref, l_ref, acc_ref, run_ref, need_ref,
                *, seq, tq, tk, tkv, n_sel):
    i = pl.program_id(1)
    nck = (i * tq) // tk + 1
    top = lax.broadcasted_iota(I32, (LANES, tq), 0) < HEAD_DIM
    half_b = (jnp.where(top, 1.0, 0.0).astype(BF16), jnp.where(top, 0.0, 1.0).astype(BF16))
    q_pos = i * tq + lax.broadcasted_iota(I32, (tk, tq), 1)
    k_off = lax.broadcasted_iota(I32, (tk, tq), 0)

    iq = iq_ref[...]
    iw = iw_ref[...]
    iqz = [iq[(hd // 2) * LANES:(hd // 2 + 1) * LANES, :] * half_b[hd % 2]
           for hd in range(N_IDX_HEADS)]
    wrow = [iw[hd:hd + 1, :] for hd in range(N_IDX_HEADS)]

    def score_chunk(c, causal):
        ikc = ik_ref[pl.ds(pl.multiple_of(c * tk, tk), tk), :]
        score = None
        for hd in range(N_IDX_HEADS):
            d = jnp.dot(ikc, iqz[hd], preferred_element_type=F32)
            t = jnp.maximum(d, 0.0) * wrow[hd]
            score = t if score is None else score + t
        bits = pltpu.bitcast(score, I32)
        neg = bits >> 31
        key = (bits ^ (neg & 0x7FFFFFFF)) - neg
        key = jnp.where(key == 0, (seq - 1 - c * tk) - k_off, key + (~neg & seq))
        if causal:
            key = jnp.where(c * tk + k_off <= q_pos, key, INT_MIN)
        hi_ref[c] = (key >> 16).astype(I16)
        lo_ref[c] = (key ^ 0x8000).astype(I16)
        return 0

    def score_pair(j, carry):
        score_chunk(2 * j, False)
        score_chunk(2 * j + 1, False)
        return carry

    lax.fori_loop(0, (nck - 1) // 2, score_pair, 0)

    @pl.when((nck - 1) % 2 == 1)
    def _():
        score_chunk(nck - 2, False)

    score_chunk(nck - 1, True)

    def count_ge(ref, cand):
        cand_b = jnp.broadcast_to(cand, (CNT_ROWS, tq)).astype(I16)
        one = jnp.ones((CNT_ROWS, tq), I16)
        zero = jnp.zeros((CNT_ROWS, tq), I16)

        def cnt_chunk(c, acc):
            a = ref[c]
            for r in range(tk // CNT_ROWS):
                acc = acc + jnp.where(a[r * CNT_ROWS:(r + 1) * CNT_ROWS, :] >= cand_b, one, zero)
            return acc

        acc = lax.fori_loop(0, nck, cnt_chunk, zero)
        return jnp.sum(acc.astype(I32).astype(F32), axis=0, keepdims=True)

    def greedy(ref, target, cnt0):
        def step(it, carry):
            t_u, cnt_t = carry
            cand_u = t_u | lax.shift_left(jnp.int32(1), 15 - it)
            cnt = count_ge(ref, cand_u + I16_MIN)
            ok = cnt >= target
            return jnp.where(ok, cand_u, t_u), jnp.where(ok, cnt, cnt_t)
        t_u, cnt_t = lax.fori_loop(0, 16, step, (jnp.zeros((1, tq), I32), cnt0))
        return t_u + I16_MIN, cnt_t

    total = jnp.full((1, tq), 1.0, F32) * (nck * tk).astype(F32)
    h_thr, cnt_ge_h = greedy(hi_ref, n_sel, total)
    h_b = jnp.broadcast_to(h_thr, (CNT_ROWS, tq)).astype(I16)

    def mask_chunk(c, above):
        hi = hi_ref[c]
        lo = lo_ref[c]
        one = jnp.ones((CNT_ROWS, tq), I16)
        zero = jnp.zeros((CNT_ROWS, tq), I16)
        for r in range(tk // CNT_ROWS):
            rows = slice(r * CNT_ROWS, (r + 1) * CNT_ROWS)
            lo_ref[c, rows, :] = jnp.where(hi[rows, :] == h_b, lo[rows, :],
                                           jnp.full((CNT_ROWS, tq), I16_MIN, I16))
            above = above + jnp.where(hi[rows, :] > h_b, one, zero)
        return above

    above = lax.fori_loop(0, nck, mask_chunk, jnp.zeros((CNT_ROWS, tq), I16))
    cnt_gt_h = jnp.sum(above.astype(I32).astype(F32), axis=0, keepdims=True)
    l_thr, cnt_ge_l = greedy(lo_ref, n_sel - cnt_gt_h, cnt_ge_h - cnt_gt_h)
    n_ge = cnt_gt_h + cnt_ge_l
    bad = jnp.maximum(jnp.abs(n_ge - n_sel), jnp.where(h_thr == I16_MIN, 1.0, 0.0))
    exact = jnp.max(bad) == 0.0

    @pl.when(jnp.logical_not(exact))
    def _():
        cnt_gt_l = jnp.where(l_thr == I16_MAX, 0.0,
                             count_ge(lo_ref, jnp.minimum(l_thr + 1, I16_MAX)))
        need_ref[...] = n_sel - cnt_gt_h - cnt_gt_l
        run_ref[...] = jnp.zeros_like(run_ref)

    q = q_ref[...]
    qz = [q[(hd // 2) * LANES:(hd // 2 + 1) * LANES, :] * half_b[hd % 2]
          for hd in range(N_ATTN_HEADS)]
    take_all = l_thr == I16_MIN
    h_sel = jnp.broadcast_to(jnp.where(take_all, h_thr - 1, h_thr), (PACK, tq)).astype(I16)
    l_sel = jnp.broadcast_to(jnp.where(take_all, I16_MIN + 1, l_thr), (PACK, tq)).astype(I16)
    k_off_v = lax.broadcasted_iota(I32, (tkv, tq), 0)
    q_pos_v = i * tq + lax.broadcasted_iota(I32, (tkv, tq), 1)

    def make_bias(c, with_ties):
        sub = pl.ds(pl.multiple_of((c % (tk // tkv)) * tkv, tkv), tkv)
        hi = hi_ref[c // (tk // tkv), sub, :]
        lo = lo_ref[c // (tk // tkv), sub, :]

        def threshold_mask():
            for r in range(tkv // PACK):
                rows = slice(r * PACK, (r + 1) * PACK)
                sel = (hi[rows, :] > h_sel) | (lo[rows, :] >= l_sel)
                bias_ref[rows, :] = jnp.where(sel, jnp.zeros((PACK, tq), BF16),
                                              jnp.full((PACK, tq), NEG_BIG, BF16)).astype(F32)

        if not with_ties:
            threshold_mask()
            return
        pl.when(exact)(threshold_mask)

        @pl.when(jnp.logical_not(exact))
        def _():
            hi32 = hi.astype(I32)
            lo32 = lo.astype(I32)
            is_h = hi32 == h_thr
            eq = is_h & (lo32 == l_thr)
            gt = (hi32 > h_thr) | (is_h & (lo32 > l_thr))
            tri = (lax.broadcasted_iota(I32, (tkv, tkv), 1)
                   <= lax.broadcasted_iota(I32, (tkv, tkv), 0))
            prefix = jnp.dot(jnp.where(tri, 1.0, 0.0).astype(BF16),
                             jnp.where(eq, 1.0, 0.0).astype(BF16),
                             preferred_element_type=F32) + run_ref[...]
            take = (gt | (eq & (prefix <= need_ref[...]))) & (c * tkv + k_off_v <= q_pos_v)
            bias_ref[...] = jnp.where(take, 0.0, NEG_BIG)
            run_ref[...] += jnp.sum(jnp.where(eq, 1.0, 0.0), axis=0, keepdims=True)

    def attention(online, skewed):
        if online:
            m_ref[...] = jnp.full_like(m_ref, NEG_BIG)
        l_ref[...] = jnp.zeros_like(l_ref)
        acc_ref[...] = jnp.zeros_like(acc_ref)

        def probs(c, slot):
            make_bias(c, with_ties=not skewed)
            start = pl.multiple_of(c * tkv, tkv)
            for hd in range(N_ATTN_HEADS):
                s = jnp.dot(k_ref[hd // 2, pl.ds(start, tkv), :], qz[hd],
                            preferred_element_type=F32) + bias_ref[...]
                if online:
                    m_old = m_ref[hd]
                    m_new = jnp.maximum(m_old, jnp.max(s, axis=0, keepdims=True))
                    alpha = jnp.exp2(m_old - m_new)
                    p = jnp.exp2(s - m_new)
                    l_ref[hd] = alpha * l_ref[hd] + jnp.sum(p, axis=0, keepdims=True)
                    acc_ref[hd] = alpha * acc_ref[hd]
                    m_ref[hd] = m_new
                else:
                    p = jnp.exp2(s)
                    l_ref[hd] += jnp.sum(p, axis=0, keepdims=True)
                p_ref[slot, hd] = p.astype(BF16)

        def weigh_values(c, slot):
            for hd in range(N_ATTN_HEADS):
                acc_ref[hd] += jnp.dot(vt_ref[c, hd * HEAD_DIM:(hd + 1) * HEAD_DIM, :],
                                       p_ref[slot, hd], preferred_element_type=F32)

        n_att = (i * tq) // tkv + 1
        if not skewed:
            def att_chunk(c, carry):
                probs(c, 0)
                weigh_values(c, 0)
                return carry

            lax.fori_loop(0, n_att, att_chunk, 0)
        else:
            def att_pair(j, carry):
                probs(2 * j + 1, 1)
                weigh_values(2 * j, 0)
                probs(2 * j + 2, 0)
                weigh_values(2 * j + 1, 1)
                return carry

            probs(0, 0)
            n_pairs = (n_att - 1) // 2
            lax.fori_loop(0, n_pairs, att_pair, 0)
            last = 2 * n_pairs

            @pl.when(last + 1 < n_att)
            def _():
                probs(last + 1, 1)
                weigh_values(last, 0)
                weigh_values(last + 1, 1)

            @pl.when(last + 1 == n_att)
            def _():
                weigh_values(last, 0)

    bounded = flag_ref[0] != 0

    @pl.when(bounded & exact)
    def _():
        attention(online=False, skewed=True)

    @pl.when(bounded & jnp.logical_not(exact))
    def _():
        attention(online=False, skewed=False)

    @pl.when(jnp.logical_not(bounded))
    def _():
        attention(online=True, skewed=False)

    for pair in range(N_ATTN_HEADS // 2):
        both = jnp.concatenate([acc_ref[2 * pair] / l_ref[2 * pair],
                                acc_ref[2 * pair + 1] / l_ref[2 * pair + 1]], axis=0)
        o_ref[:, pair * LANES:(pair + 1) * LANES] = both.T.astype(BF16)


def _dsa_call(flag, q, k, vt, iq, ik, iwt, *, tq, tk):
    b, _, s, _ = k.shape
    tkv = vt.shape[-1]
    n_sel = min(TOPK_MAX, s // 4)
    tile_t = lambda w: pl.BlockSpec((None, w, tq), lambda bi, i, f: (bi, 0, i))
    whole = lambda *shape: pl.BlockSpec((None,) + shape, lambda bi, i, f: (bi,) + (0,) * len(shape),
                                        pipeline_mode=pl.Buffered(1))
    grid_spec = pltpu.PrefetchScalarGridSpec(
        num_scalar_prefetch=1,
        grid=(b, s // tq),
        in_specs=[tile_t(ATTN_W), tile_t(IDX_W), tile_t(SUBLANES),
                  whole(N_ATTN_HEADS // 2, s, LANES), whole(s // tkv, ATTN_W, tkv),
                  whole(s, LANES)],
        out_specs=pl.BlockSpec((None, tq, ATTN_W), lambda bi, i, f: (bi, i, 0)),
        scratch_shapes=[
            pltpu.VMEM((s // tk, tk, tq), I16),
            pltpu.VMEM((s // tk, tk, tq), I16),
            pltpu.VMEM((tkv, tq), F32),
            pltpu.VMEM((2, N_ATTN_HEADS, tkv, tq), BF16),
            pltpu.VMEM((N_ATTN_HEADS, 1, tq), F32),
            pltpu.VMEM((N_ATTN_HEADS, 1, tq), F32),
            pltpu.VMEM((N_ATTN_HEADS, HEAD_DIM, tq), F32),
            pltpu.VMEM((1, tq), F32),
            pltpu.VMEM((1, tq), F32),
        ])
    return pl.pallas_call(
        functools.partial(_dsa_kernel, seq=s, tq=tq, tk=tk, tkv=tkv, n_sel=float(n_sel)),
        grid_spec=grid_spec,
        out_shape=jax.ShapeDtypeStruct((b, s, ATTN_W), BF16),
        compiler_params=pltpu.CompilerParams(
            dimension_semantics=("arbitrary", "arbitrary"), vmem_limit_bytes=VMEM_LIMIT_BYTES),
        name="dsa",
    )(flag, q, iq, iwt, k, vt, ik)


def _prep_w_in(w_in):
    offs = [sum(IN_SIZES[:n + 1]) for n in range(len(IN_SIZES) - 1)]
    aq, ak, av, iq, ik, iw, pu, mq = jnp.split(w_in, offs, axis=-1)
    pad = jnp.zeros(w_in.shape[:-1] + (LANES - N_IDX_HEADS,), w_in.dtype)
    return jnp.concatenate([aq, ak, av, iq, ik, ik, pu, mq, iw, pad], axis=-1).astype(BF16)


def _block_diag(w_pool):
    depth, g, c, _ = w_pool.shape
    eye = jnp.eye(g, dtype=w_pool.dtype)
    return jnp.einsum("lgcd,gh->lgchd", w_pool, eye).reshape(depth, g * c, g * c).astype(BF16)


def kernel(x, mem, ffn1_norm, ffn1_gate, ffn1_up, ffn1_down, mix_norm, mem_norm, w_in, q_norm,
           k_norm, w_pool, pool_scale, w_mem_kv, mq_norm, mk_norm, w_out, ffn2_norm, ffn2_gate,
           ffn2_up, ffn2_down):
    b, s, d = x.shape
    depth = w_in.shape[0]
    tm = min(512, s)
    tq = min(256, s)
    tk = min(512, s)
    tkv = min(512, s)
    assert s % tm == 0 and s % tk == 0 and tk % tq == 0 and tk % tkv == 0 and tm % tkv == 0

    vec = lambda a, reps=1: jnp.tile(a, (1, reps))[:, None, :]
    ffn1 = tuple(w.astype(BF16) for w in (ffn1_gate, ffn1_up, ffn1_down))
    ffn2 = tuple(w.astype(BF16) for w in (ffn2_gate, ffn2_up, ffn2_down))
    w_in_p = _prep_w_in(w_in)
    w_out_b = w_out.astype(BF16)
    w_pool_bd = _block_diag(w_pool)
    head = jnp.arange(MXU_N) // HEAD_DIM
    bd = jnp.where(head[:, None] == head[None, :], 1.0 / HEAD_DIM, 0.0).astype(BF16)
    qg, kg = vec(q_norm, N_ATTN_HEADS), vec(k_norm, N_ATTN_HEADS)
    mqg, mkg = vec(mq_norm, N_MEM_HEADS), vec(mk_norm, N_MEM_HEADS)
    f1g, f2g, mixg, memg, ps = vec(ffn1_norm), vec(ffn2_norm), vec(mix_norm), vec(mem_norm), vec(pool_scale)
    log2_bound = (NORM_SLACK * LOG2E * jnp.max(jnp.abs(q_norm), axis=-1)
                  * jnp.max(jnp.abs(k_norm), axis=-1))
    bounded = (log2_bound <= SAFE_LOG2_BOUND).astype(I32)

    mk, mv = _memkv_call(mem, memg, w_mem_kv.astype(BF16), mkg, bd)

    x2d = x.reshape(b * s, d)
    for l in range(depth):
        x2d = _ffn_call(x2d, l, f1g, *ffn1, tm=tm, fc=MXU_N)
        q, k, vt, iq, ik, iwt, y_pm = _proj_call(
            x2d.reshape(b, s, d), l, mixg, w_in_p, qg, kg, mqg, bd, w_pool_bd, ps, mk, mv,
            tm=tm, tkv=tkv)
        y_attn = _dsa_call(bounded[l:l + 1], q, k, vt, iq, ik, iwt, tq=tq, tk=tk)
        x2d = _ffn_call(x2d, l, f2g, *ffn2, tm=tm, fc=MXU_N,
                        mix=(y_attn.reshape(b * s, ATTN_W), y_pm.reshape(b * s, POOL_W + MEM_W),
                             w_out_b))
    return x2d.reshape(b, s, d)
```
